```python
import jax, jax.numpy as jnp
from jax import lax
import numpy as np

D_MODEL = 1024
BATCH = 4
SEQ = 8192
DEPTH = 1

D_MIX = D_MODEL
HEAD_DIM = 64
MOBA_HEADS = (D_MIX // 2) // HEAD_DIM
MOBA_WIDTH = MOBA_HEADS * HEAD_DIM
DIFF_HEADS = (D_MIX // 2) // (2 * HEAD_DIM)
DIFF_WIDTH = DIFF_HEADS * 2 * HEAD_DIM
MOBA_BLOCK = 256
MOBA_TOPK = 3
MOBA_Q_CHUNK = 64
DIFF_Q_CHUNK = 128
IN_COLS = 3 * MOBA_WIDTH + 3 * DIFF_WIDTH + D_MIX
EPS = 1e-6

kernel_name = "moba_diffattn_parallel_hybrid_adaln"


def rmsnorm(x, g):
    xf = x.astype(jnp.float32)
    y = xf * lax.rsqrt(jnp.mean(xf * xf, axis=-1, keepdims=True) + EPS)
    return (y * g.astype(jnp.float32)).astype(x.dtype)


def alibi_slopes(n):
    return jnp.asarray(2.0 ** (-8.0 * np.arange(1, n + 1, dtype=np.float32) / n), dtype=jnp.float32)


def moba_attention(q, k, v, slopes):
    B, H, S, Dh = q.shape
    s_pad = ((S + MOBA_BLOCK - 1) // MOBA_BLOCK) * MOBA_BLOCK
    pad = [(0, 0), (0, 0), (0, s_pad - S), (0, 0)]
    k_pad = jnp.pad(k, pad)
    v_pad = jnp.pad(v, pad)
    nb = s_pad // MOBA_BLOCK
    kk = min(MOBA_TOPK, nb)
    k_blk = k_pad.reshape(B, H, nb, MOBA_BLOCK, Dh)
    v_blk = v_pad.reshape(B, H, nb, MOBA_BLOCK, Dh)
    k_mean = jnp.mean(k_blk.astype(jnp.float32), axis=3)
    scale = Dh ** -0.5
    n_chunks = S // MOBA_Q_CHUNK
    q_chunks = q.reshape(B, H, n_chunks, MOBA_Q_CHUNK, Dh).transpose(2, 0, 1, 3, 4)
    bi = jnp.arange(B)[:, None, None, None]
    hi = jnp.arange(H)[None, :, None, None]
    blk_off = jnp.arange(MOBA_BLOCK)
    blk_ids = jnp.arange(nb)
    m = slopes[None, :, None, None]

    def chunk_fn(args):
        ci, qc = args
        start = ci * MOBA_Q_CHUNK
        t = start + jnp.arange(MOBA_Q_CHUNK)
        cur = start // MOBA_BLOCK
        qf = qc.astype(jnp.float32)
        gate = jnp.einsum('bhcd,bhnd->bhcn', qf, k_mean)
        gate = jnp.where(blk_ids < cur, gate, -jnp.inf)
        _, idx = lax.top_k(gate, kk)
        valid = idx < cur
        ks = k_blk[bi, hi, idx].astype(jnp.float32)
        vs = v_blk[bi, hi, idx].astype(jnp.float32)
        s_sel = jnp.einsum('bhcd,bhcjkd->bhcjk', qf, ks) * scale
        pos_sel = idx[..., None] * MOBA_BLOCK + blk_off
        s_sel = s_sel - m[..., None] * (t[None, None, :, None, None] - pos_sel).astype(jnp.float32)
        s_sel = jnp.where(valid[..., None], s_sel, -jnp.inf)
        k_own = lax.dynamic_slice_in_dim(k_pad, cur * MOBA_BLOCK, MOBA_BLOCK, axis=2).astype(jnp.float32)
        v_own = lax.dynamic_slice_in_dim(v_pad, cur * MOBA_BLOCK, MOBA_BLOCK, axis=2).astype(jnp.float32)
        s_own = jnp.einsum('bhcd,bhkd->bhck', qf, k_own) * scale
        dist_own = t[:, None] - (cur * MOBA_BLOCK + blk_off)[None, :]
        s_own = s_own - m * dist_own.astype(jnp.float32)
        s_own = jnp.where(dist_own >= 0, s_own, -jnp.inf)
        logits = jnp.concatenate([s_sel.reshape(B, H, MOBA_Q_CHUNK, kk * MOBA_BLOCK), s_own], axis=-1)
        p = jax.nn.softmax(logits, axis=-1)
        p_sel = p[..., :kk * MOBA_BLOCK].reshape(B, H, MOBA_Q_CHUNK, kk, MOBA_BLOCK)
        p_own = p[..., kk * MOBA_BLOCK:]
        out = jnp.einsum('bhcjk,bhcjkd->bhcd', p_sel, vs) + jnp.einsum('bhck,bhkd->bhcd', p_own, v_own)
        return out.astype(q.dtype)

    out = lax.map(chunk_fn, (jnp.arange(n_chunks), q_chunks))
    return out.transpose(1, 2, 0, 3, 4).reshape(B, H, S, Dh)


def diff_attention(q, k, v, lam, lam_init, subln_g, slopes):
    B, H, _, S, Dh = q.shape
    scale = Dh ** -0.5
    n_chunks = S // DIFF_Q_CHUNK
    q_chunks = q.reshape(B, H, 2, n_chunks, DIFF_Q_CHUNK, Dh).transpose(3, 0, 1, 2, 4, 5)
    kf = k.astype(jnp.float32)
    vf = v.astype(jnp.float32)
    s_pos = jnp.arange(S)
    m = slopes[None, :, None, None, None]

    def chunk_fn(args):
        ci, qc = args
        t = ci * DIFF_Q_CHUNK + jnp.arange(DIFF_Q_CHUNK)
        s = jnp.einsum('bhicd,bhisd->bhics', qc.astype(jnp.float32), kf) * scale
        dist = t[:, None] - s_pos[None, :]
        s = s - m * dist.astype(jnp.float32)
        s = jnp.where(dist >= 0, s, -jnp.inf)
        p = jax.nn.softmax(s, axis=-1)
        a = p[:, :, 0] - lam * p[:, :, 1]
        return jnp.einsum('bhcs,bhse->bhce', a, vf)

    out = lax.map(chunk_fn, (jnp.arange(n_chunks), q_chunks))
    out = out.transpose(1, 2, 0, 3, 4).reshape(B, H, S, 2 * Dh)
    out = rmsnorm(out, subln_g) * (1.0 - lam_init)
    return out.astype(q.dtype)


def setup_inputs(seed: int = 0) -> dict:
    key = jax.random.key(seed)
    ks = jax.random.split(key, 14)
    f32 = jnp.float32
    x = jax.random.normal(ks[0], (BATCH, SEQ, D_MODEL), f32)
    c = jax.random.normal(ks[1], (BATCH, D_MODEL), f32)
    ln_g = 1.0 + 0.02 * jax.random.normal(ks[2], (DEPTH, D_MODEL), f32)
    w_ada = jax.random.normal(ks[3], (DEPTH, D_MODEL, 3 * D_MODEL), f32) * D_MODEL ** -0.5
    b_ada = 0.02 * jax.random.normal(ks[4], (DEPTH, 3 * D_MODEL), f32)
    w_in = jax.random.normal(ks[5], (DEPTH, D_MODEL, IN_COLS), f32) * D_MODEL ** -0.5
    lambda_q1 = 0.1 * jax.random.normal(ks[6], (DEPTH, HEAD_DIM), f32)
    lambda_k1 = 0.1 * jax.random.normal(ks[7], (DEPTH, HEAD_DIM), f32)
    lambda_q2 = 0.1 * jax.random.normal(ks[8], (DEPTH, HEAD_DIM), f32)
    lambda_k2 = 0.1 * jax.random.normal(ks[9], (DEPTH, HEAD_DIM), f32)
    subln_g = 1.0 + 0.02 * jax.random.normal(ks[10], (DEPTH, 2 * HEAD_DIM), f32)
    w_out = jax.random.normal(ks[11], (DEPTH, D_MIX, D_MODEL), f32) * D_MIX ** -0.5
    final_g = 1.0 + 0.02 * jax.random.normal(ks[12], (D_MODEL,), f32)
    return {"x": x, "c": c, "ln_g": ln_g, "w_ada": w_ada, "b_ada": b_ada, "w_in": w_in,
            "lambda_q1": lambda_q1, "lambda_k1": lambda_k1, "lambda_q2": lambda_q2, "lambda_k2": lambda_k2,
            "subln_g": subln_g, "w_out": w_out, "final_g": final_g}


def reference(x, c, ln_g, w_ada, b_ada, w_in, lambda_q1, lambda_k1, lambda_q2, lambda_k2,
              subln_g, w_out, final_g):
    B, S, D = x.shape
    moba_slopes = alibi_slopes(MOBA_HEADS)
    diff_slopes = alibi_slopes(DIFF_HEADS)
    c_act = jax.nn.silu(c)
    for l in range(DEPTH):
        mod = c_act @ w_ada[l] + b_ada[l]
        shift, scl, gate = jnp.split(mod, 3, axis=-1)
        h = rmsnorm(x, ln_g[l]) * (1.0 + scl[:, None, :]) + shift[:, None, :]
        proj = h @ w_in[l]
        sizes = np.cumsum([MOBA_WIDTH] * 3 + [DIFF_WIDTH] * 3)
        mq, mk, mv, dq, dk, dv, z = jnp.split(proj, sizes, axis=-1)
        to_heads = lambda a, H: a.reshape(B, S, H, -1).transpose(0, 2, 1, 3)
        a_out = moba_attention(to_heads(mq, MOBA_HEADS), to_heads(mk, MOBA_HEADS),
                               to_heads(mv, MOBA_HEADS), moba_slopes)
        a_out = a_out.transpose(0, 2, 1, 3).reshape(B, S, MOBA_WIDTH)
        dq_h = dq.reshape(B, S, DIFF_HEADS, 2, HEAD_DIM).transpose(0, 2, 3, 1, 4)
        dk_h = dk.reshape(B, S, DIFF_HEADS, 2, HEAD_DIM).transpose(0, 2, 3, 1, 4)
        dv_h = to_heads(dv, DIFF_HEADS)
        lam_init = 0.8 - 0.6 * float(np.exp(-0.3 * l))
        lam = (jnp.exp(jnp.sum(lambda_q1[l].astype(jnp.float32) * lambda_k1[l].astype(jnp.float32)))
               - jnp.exp(jnp.sum(lambda_q2[l].astype(jnp.float32) * lambda_k2[l].astype(jnp.float32)))
               + lam_init)
        b_out = diff_attention(dq_h, dk_h, dv_h, lam, lam_init, subln_g[l], diff_slopes)
        b_out = b_out.transpose(0, 2, 1, 3).reshape(B, S, DIFF_WIDTH)
        y = jnp.concatenate([a_out, b_out], axis=-1) * jax.nn.silu(z)
        x = x + gate[:, None, :] * (y @ w_out[l])
    return rmsnorm(x, final_g)
```

```python
import functools

import numpy as np
import jax
import jax.numpy as jnp
from jax import lax
from jax.experimental import pallas as pl
from jax.experimental.pallas import tpu as pltpu

F32 = jnp.float32
BF16 = jnp.bfloat16

HEAD_DIM = 64
MOBA_HEADS = 8
DIFF_HEADS = 4
MOBA_BLOCK = 256
MOBA_TOPK = 3
EPS = 1e-6
LAM_INIT = 0.8 - 0.6 * float(np.exp(-0.3 * 0))

BLK = MOBA_BLOCK
LANES = 128
POS_SPLIT = 64
NEG = -1e30
VMEM_LIMIT = 48 * 1024 * 1024


def _alibi_slopes(n):
    return [2.0 ** (-8.0 * (i + 1) / n) for i in range(n)]


def _sigmoid(v):
    return 1.0 / (1.0 + jnp.exp(-v))


def _mod_kernel(c_ref, w_ref, b_ref, lv_ref, mod_ref, lam_ref):
    c = c_ref[...]
    ca = c * _sigmoid(c)
    ca_hi = ca.astype(BF16)
    ca_lo = (ca - ca_hi.astype(F32)).astype(BF16)
    w = w_ref[...]
    w_hi = w.astype(BF16)
    w_lo = (w - w_hi.astype(F32)).astype(BF16)
    acc = jnp.dot(ca_hi, w_hi, preferred_element_type=F32)
    acc += jnp.dot(ca_lo, w_hi, preferred_element_type=F32)
    acc += jnp.dot(ca_hi, w_lo, preferred_element_type=F32)
    mod_ref[...] = acc + b_ref[...]
    lv = lv_ref[...]
    s1 = jnp.sum(lv[0:1, :] * lv[1:2, :], axis=1, keepdims=True)
    s2 = jnp.sum(lv[2:3, :] * lv[3:4, :], axis=1, keepdims=True)
    lam = jnp.exp(s1) - jnp.exp(s2) + LAM_INIT
    lam_ref[...] = jnp.broadcast_to(lam, lam_ref.shape)


def _mod_call(c8, w_ada, b_ada, lvec):
    d = c8.shape[1]
    n = w_ada.shape[1]
    tn = 512
    return pl.pallas_call(
        _mod_kernel,
        grid=(n // tn,),
        in_specs=[
            pl.BlockSpec((8, d), lambda j: (0, 0)),
            pl.BlockSpec((d, tn), lambda j: (0, j)),
            pl.BlockSpec((1, tn), lambda j: (0, j)),
            pl.BlockSpec((8, LANES), lambda j: (0, 0)),
        ],
        out_specs=[
            pl.BlockSpec((8, tn), lambda j: (0, j)),
            pl.BlockSpec((8, LANES), lambda j: (0, 0)),
        ],
        out_shape=[
            jax.ShapeDtypeStruct((8, n), F32),
            jax.ShapeDtypeStruct((8, LANES), F32),
        ],
        compiler_params=pltpu.CompilerParams(dimension_semantics=("arbitrary",)),
        name="adaln_mod",
    )(c8, w_ada, b_ada, lvec)


def _proj_kernel(x_ref, mod_ref, g_ref, wa_ref, wbt_ref,
                 km_ref, kd_ref, z_ref, qm_ref, qd_ref, vm_ref, vd_ref, kmean_ref):
    x = x_ref[0]
    ms = jnp.mean(x * x, axis=-1, keepdims=True)
    y = x * lax.rsqrt(ms + EPS) * g_ref[...]
    shift = mod_ref[0, 0:1, :]
    scl = mod_ref[0, 1:2, :]
    h = (y * (1.0 + scl) + shift).astype(BF16)

    pa = jnp.dot(h, wa_ref[...], preferred_element_type=F32)
    kw = MOBA_HEADS * HEAD_DIM
    for p in range(kw // LANES):
        km_ref[0, p, 0] = pa[:, p * LANES:(p + 1) * LANES].astype(BF16)
        kd_ref[0, p, 0] = pa[:, kw + p * LANES: kw + (p + 1) * LANES].astype(BF16)
    z_ref[0] = pa[:, 2 * kw:].astype(BF16)
    kmean_ref[0, 0] = jnp.mean(pa[:, :kw], axis=0, keepdims=True)

    pb = lax.dot_general(wbt_ref[...], h, (((1,), (1,)), ((), ())), preferred_element_type=F32)
    qm_ref[0, 0] = pb[0 * kw:1 * kw].astype(BF16)
    qd_ref[0, 0] = pb[1 * kw:2 * kw].astype(BF16)
    vm_ref[0, 0] = pb[2 * kw:3 * kw].astype(BF16)
    vd_ref[0, 0] = pb[3 * kw:4 * kw].astype(BF16)


def _proj_call(x, mod3, ln_g, wa, wbt):
    b, s, d = x.shape
    nb = s // BLK
    kw = MOBA_HEADS * HEAD_DIM
    nsl = kw // LANES
    kslab = jax.ShapeDtypeStruct((b, nsl, nb, BLK, LANES), BF16)
    tslab = jax.ShapeDtypeStruct((b, nb, kw, BLK), BF16)
    kspec = pl.BlockSpec((1, nsl, 1, BLK, LANES), lambda i, t: (i, 0, t, 0, 0))
    tspec = pl.BlockSpec((1, 1, kw, BLK), lambda i, t: (i, t, 0, 0))
    return pl.pallas_call(
        _proj_kernel,
        grid=(b, nb),
        in_specs=[
            pl.BlockSpec((1, BLK, d), lambda i, t: (i, t, 0)),
            pl.BlockSpec((1, 3, d), lambda i, t: (i, 0, 0)),
            pl.BlockSpec((1, d), lambda i, t: (0, 0)),
            pl.BlockSpec(wa.shape, lambda i, t: (0, 0)),
            pl.BlockSpec(wbt.shape, lambda i, t: (0, 0)),
        ],
        out_specs=[
            kspec, kspec,
            pl.BlockSpec((1, BLK, d), lambda i, t: (i, t, 0)),
            tspec, tspec, tspec, tspec,
            pl.BlockSpec((1, 1, 1, kw), lambda i, t: (i, t, 0, 0)),
        ],
        out_shape=[
            kslab, kslab,
            jax.ShapeDtypeStruct((b, s, d), BF16),
            tslab, tslab, tslab, tslab,
            jax.ShapeDtypeStruct((b, nb, 1, kw), F32),
        ],
        compiler_params=pltpu.CompilerParams(
            dimension_semantics=("arbitrary", "arbitrary"), vmem_limit_bytes=VMEM_LIMIT),
        name="norm_in_proj",
    )(x, mod3, ln_g, wa, wbt)


def _sel_kernel(km_ref, q_ref, sb_ref):
    t = pl.program_id(2)
    km = km_ref[0, 0]
    km_hi = km.astype(BF16)
    km_lo = (km - km_hi.astype(F32)).astype(BF16)
    q = q_ref[0, 0]
    gate = (jnp.dot(km_hi, q, preferred_element_type=F32)
            + jnp.dot(km_lo, q, preferred_element_type=F32))
    nb = gate.shape[0]
    blk = lax.broadcasted_iota(jnp.int32, gate.shape, 0)
    valid = blk < t
    g = jnp.where(valid, gate, -jnp.inf)
    sel = jnp.zeros(gate.shape, jnp.bool_)
    for _ in range(MOBA_TOPK):
        mx = jnp.max(g, axis=0, keepdims=True)
        first = jnp.min(jnp.where(g == mx, blk, nb), axis=0, keepdims=True)
        hit = blk == first
        sel = jnp.logical_or(sel, hit)
        g = jnp.where(hit, -jnp.inf, g)
    sb_ref[0, 0] = jnp.where(jnp.logical_and(sel, valid), 0.0, NEG)


def _sel_call(kmean_h, qt_m):
    b, nh, nb, hd = kmean_h.shape
    return pl.pallas_call(
        _sel_kernel,
        grid=(b, nh, nb),
        in_specs=[
            pl.BlockSpec((1, 1, nb, hd), lambda i, h, t: (i, h, 0, 0)),
            pl.BlockSpec((1, 1, hd, BLK), lambda i, h, t: (i, t, h, 0)),
        ],
        out_specs=pl.BlockSpec((1, 1, nb, BLK), lambda i, h, t: (i, h, 0, t)),
        out_shape=jax.ShapeDtypeStruct((b, nh, nb, nb * BLK), F32),
        compiler_params=pltpu.CompilerParams(dimension_semantics=("arbitrary",) * 3),
        name="moba_select",
    )(kmean_h, qt_m)


def _one_rows(first_row):
    r = lax.broadcasted_iota(jnp.int32, (LANES, BLK), 0)
    return jnp.where(jnp.logical_and(r >= first_row, r < first_row + 2), 1.0, 0.0).astype(BF16)


def _causal_mask():
    r = lax.broadcasted_iota(jnp.int32, (BLK, 2 * BLK), 0)
    c = lax.broadcasted_iota(jnp.int32, (BLK, 2 * BLK), 1)
    return r <= jnp.bitwise_and(c, BLK - 1)


def _moba_kernel(q_ref, k_ref, aug_ref, v_ref, sb_ref, o_ref):
    qi = pl.program_id(2)
    q = q_ref[0, 0]
    zero = jnp.zeros((HEAD_DIM, BLK), BF16)
    qp = jnp.concatenate([
        jnp.concatenate([q[0:HEAD_DIM], zero, _one_rows(0)], axis=0),
        jnp.concatenate([zero, q[HEAD_DIM:], _one_rows(2)], axis=0)], axis=1)

    def scores(j):
        kd = jnp.concatenate([k_ref[0, 0, j], aug_ref[0, j]], axis=1)
        return jnp.dot(kd, qp, preferred_element_type=F32)

    def pv(j, p):
        v = v_ref[0, j]
        pb = p.astype(BF16)
        return (jnp.dot(v[0:HEAD_DIM], pb[:, :BLK], preferred_element_type=F32),
                jnp.dot(v[HEAD_DIM:], pb[:, BLK:], preferred_element_type=F32))

    s = jnp.where(_causal_mask(), scores(qi), NEG)
    m = jnp.max(s, axis=0, keepdims=True)
    p = jnp.exp(s - m)
    l = jnp.sum(p, axis=0, keepdims=True)
    acc0, acc1 = pv(qi, p)

    def body(j, carry):
        m, l, acc0, acc1 = carry
        s = scores(j)
        bias = jnp.concatenate([sb_ref[0, 0, pl.ds(j, 1), :], sb_ref[0, 1, pl.ds(j, 1), :]], axis=1)
        m_new = jnp.maximum(m, jnp.max(s, axis=0, keepdims=True) + bias)
        alpha = jnp.exp(m - m_new)
        p = jnp.exp(s + (bias - m_new))
        l = alpha * l + jnp.sum(p, axis=0, keepdims=True)
        d0, d1 = pv(j, p)
        return m_new, l, alpha[:, :BLK] * acc0 + d0, alpha[:, BLK:] * acc1 + d1

    m, l, acc0, acc1 = lax.fori_loop(0, qi, body, (m, l, acc0, acc1))
    out = jnp.concatenate([acc0 / l[:, :BLK], acc1 / l[:, BLK:]], axis=0)
    o_ref[0] = out.T.astype(BF16)


def _moba_call(qt_m, k_m, aug_m, vt_m, selbias):
    b, nb, kw, _ = qt_m.shape
    npair = kw // LANES
    s = nb * BLK
    return pl.pallas_call(
        _moba_kernel,
        grid=(b, npair, nb),
        in_specs=[
            pl.BlockSpec((1, 1, LANES, BLK), lambda i, p, t: (i, t, p, 0)),
            pl.BlockSpec((1, 1, nb, BLK, LANES), lambda i, p, t: (i, p, 0, 0, 0)),
            pl.BlockSpec((1, nb, BLK, LANES), lambda i, p, t: (p, 0, 0, 0)),
            pl.BlockSpec((1, nb, LANES, BLK), lambda i, p, t: (i, 0, p, 0)),
            pl.BlockSpec((1, 2, nb, BLK), lambda i, p, t: (i, p, 0, t)),
        ],
        out_specs=pl.BlockSpec((1, BLK, LANES), lambda i, p, t: (i, t, p)),
        out_shape=jax.ShapeDtypeStruct((b, s, kw), BF16),
        compiler_params=pltpu.CompilerParams(
            dimension_semantics=("arbitrary",) * 3, vmem_limit_bytes=VMEM_LIMIT),
        name="moba_attn",
    )(qt_m, k_m, aug_m, vt_m, selbias)


def _diff_kernel(lam_ref, q_ref, k_ref, aug_ref, v_ref, g_ref, o_ref):
    qi = pl.program_id(2)
    q = q_ref[0, 0]
    zero = jnp.zeros((HEAD_DIM, BLK), BF16)
    ones = _one_rows(0)
    qp = jnp.concatenate([
        jnp.concatenate([q[0:HEAD_DIM], zero, ones], axis=0),
        jnp.concatenate([zero, q[HEAD_DIM:], ones], axis=0)], axis=1)

    def scores(j):
        kd = jnp.concatenate([k_ref[0, 0, j], aug_ref[0, j]], axis=1)
        return jnp.dot(kd, qp, preferred_element_type=F32)

    def pv(j, p):
        return jnp.dot(v_ref[0, j], p.astype(BF16), preferred_element_type=F32)

    s = jnp.where(_causal_mask(), scores(qi), NEG)
    m = jnp.max(s, axis=0, keepdims=True)
    p = jnp.exp(s - m)
    l = jnp.sum(p, axis=0, keepdims=True)
    acc = pv(qi, p)

    def body(j, carry):
        m, l, acc = carry
        s = scores(j)
        m_new = jnp.maximum(m, jnp.max(s, axis=0, keepdims=True))
        alpha = jnp.exp(m - m_new)
        p = jnp.exp(s - m_new)
        l = alpha * l + jnp.sum(p, axis=0, keepdims=True)
        return m_new, l, alpha * acc + pv(j, p)

    m, l, acc = lax.fori_loop(0, qi, body, (m, l, acc))
    o = acc / l
    d = o[:, :BLK] - lam_ref[0, 0] * o[:, BLK:]
    ms = jnp.mean(d * d, axis=0, keepdims=True)
    y = d * lax.rsqrt(ms + EPS) * g_ref[...] * (1.0 - LAM_INIT)
    o_ref[0] = y.T.astype(BF16)


def _diff_call(lam, qt_d, k_d, aug_d, vt_d, subln_col):
    b, nb, kw, _ = qt_d.shape
    nh = kw // LANES
    s = nb * BLK
    return pl.pallas_call(
        _diff_kernel,
        grid=(b, nh, nb),
        in_specs=[
            pl.BlockSpec(memory_space=pltpu.SMEM),
            pl.BlockSpec((1, 1, LANES, BLK), lambda i, h, t: (i, t, h, 0)),
            pl.BlockSpec((1, 1, nb, BLK, LANES), lambda i, h, t: (i, h, 0, 0, 0)),
            pl.BlockSpec((1, nb, BLK, LANES), lambda i, h, t: (h, 0, 0, 0)),
            pl.BlockSpec((1, nb, LANES, BLK), lambda i, h, t: (i, 0, h, 0)),
            pl.BlockSpec((LANES, 1), lambda i, h, t: (0, 0)),
        ],
        out_specs=pl.BlockSpec((1, BLK, LANES), lambda i, h, t: (i, t, h)),
        out_shape=jax.ShapeDtypeStruct((b, s, kw), BF16),
        compiler_params=pltpu.CompilerParams(
            dimension_semantics=("arbitrary",) * 3, vmem_limit_bytes=VMEM_LIMIT),
        name="diff_attn",
    )(lam, qt_d, k_d, aug_d, vt_d, subln_col)


def _out_kernel(a_ref, b_ref, z_ref, x_ref, mod_ref, w_ref, g_ref, o_ref):
    z = z_ref[0].astype(F32)
    heads = jnp.concatenate([a_ref[0], b_ref[0]], axis=-1).astype(F32)
    y = (heads * (z * _sigmoid(z))).astype(BF16)
    r = jnp.dot(y, w_ref[...], preferred_element_type=F32)
    xo = x_ref[0] + mod_ref[0, 2:3, :] * r
    ms = jnp.mean(xo * xo, axis=-1, keepdims=True)
    o_ref[0] = xo * lax.rsqrt(ms + EPS) * g_ref[...]


def _out_call(a_out, b_out, z, x, mod3, w_out, final_g):
    b, s, d = x.shape
    tm = 512
    kw = a_out.shape[-1]
    return pl.pallas_call(
        _out_kernel,
        grid=(b, s // tm),
        in_specs=[
            pl.BlockSpec((1, tm, kw), lambda i, t: (i, t, 0)),
            pl.BlockSpec((1, tm, kw), lambda i, t: (i, t, 0)),
            pl.BlockSpec((1, tm, d), lambda i, t: (i, t, 0)),
            pl.BlockSpec((1, tm, d), lambda i, t: (i, t, 0)),
            pl.BlockSpec((1, 3, d), lambda i, t: (i, 0, 0)),
            pl.BlockSpec(w_out.shape, lambda i, t: (0, 0)),
            pl.BlockSpec((1, d), lambda i, t: (0, 0)),
        ],
        out_specs=pl.BlockSpec((1, tm, d), lambda i, t: (i, t, 0)),
        out_shape=jax.ShapeDtypeStruct((b, s, d), F32),
        compiler_params=pltpu.CompilerParams(
            dimension_semantics=("arbitrary", "arbitrary"), vmem_limit_bytes=VMEM_LIMIT),
        name="gate_out_proj_norm",
    )(a_out, b_out, z, x, mod3, w_out, final_g)


def _alibi_columns(slopes_per_slab, s):
    pos = np.arange(s)
    hi = (pos // POS_SPLIT) * POS_SPLIT
    lo = pos % POS_SPLIT
    out = np.zeros((len(slopes_per_slab), s, LANES), np.float32)
    for n, slopes in enumerate(slopes_per_slab):
        for i, m in enumerate(slopes):
            out[n, :, 2 * i] = m * hi
            out[n, :, 2 * i + 1] = m * lo
    return jnp.asarray(out.reshape(len(slopes_per_slab), s // BLK, BLK, LANES), dtype=BF16)


def _forward(x, c, ln_g, w_ada, b_ada, w_in, lambda_q1, lambda_k1, lambda_q2, lambda_k2, subln_g, w_out, final_g):
    b, s, d = x.shape
    kw = MOBA_HEADS * HEAD_DIM
    assert s % BLK == 0 and d == 2 * kw and w_in.shape[-1] == 6 * kw + d

    c8 = jnp.pad(c, ((0, 8 - b), (0, 0)))
    lvec = jnp.pad(jnp.stack([lambda_q1[0], lambda_k1[0], lambda_q2[0], lambda_k2[0]]),
                   ((0, 4), (0, LANES - HEAD_DIM)))
    mod, lam = _mod_call(c8, w_ada[0], b_ada[0][None, :], lvec)
    mod3 = mod[:b].reshape(b, 3, d)
    lam = lam[0:1, 0:1]

    w = w_in[0]
    mq, mk, mv, dq, dk, dv = (w[:, i * kw:(i + 1) * kw] for i in range(6))
    wz = w[:, 6 * kw:]
    scale = HEAD_DIM ** -0.5
    wa = jnp.concatenate([mk, dk, wz], axis=1).astype(BF16)
    wbt = jnp.concatenate([mq * scale, dq * scale, mv, dv], axis=1).T.astype(BF16)

    k_m, k_d, z, qt_m, qt_d, vt_m, vt_d, kmean = _proj_call(x, mod3, ln_g[0][None, :], wa, wbt)

    nb = s // BLK
    kmean_h = kmean.reshape(b, nb, MOBA_HEADS, HEAD_DIM).transpose(0, 2, 1, 3)
    selbias = _sel_call(kmean_h, qt_m)

    ms = _alibi_slopes(MOBA_HEADS)
    aug_m = _alibi_columns([ms[2 * p:2 * p + 2] for p in range(MOBA_HEADS // 2)], s)
    aug_d = _alibi_columns([[m] for m in _alibi_slopes(DIFF_HEADS)], s)

    a_out = _moba_call(qt_m, k_m, aug_m, vt_m, selbias)
    b_out = _diff_call(lam, qt_d, k_d, aug_d, vt_d, subln_g[0][:, None])
    out = _out_call(a_out, b_out, z, x, mod3, w_out[0].astype(BF16), final_g[None, :])
    return out, a_out, b_out


def kernel(x, c, ln_g, w_ada, b_ada, w_in, lambda_q1, lambda_k1, lambda_q2, lambda_k2, subln_g, w_out, final_g):
    return _forward(x, c, ln_g, w_ada, b_ada, w_in, lambda_q1, lambda_k1, lambda_q2, lambda_k2,
                    subln_g, w_out, final_g)[0]
```

```python
import functools

import numpy as np
import jax
import jax.numpy as jnp
from jax import lax
from jax.experimental import pallas as pl
from jax.experimental.pallas import tpu as pltpu

F32 = jnp.float32
BF16 = jnp.bfloat16

HEAD_DIM = 64
MOBA_HEADS = 8
DIFF_HEADS = 4
MOBA_BLOCK = 256
MOBA_TOPK = 3
EPS = 1e-6
LAM_INIT = 0.8 - 0.6 * float(np.exp(-0.3 * 0))

BLK = MOBA_BLOCK
LANES = 128
POS_SPLIT = 64
NEG = -1e30
VMEM_LIMIT = 48 * 1024 * 1024


def _alibi_slopes(n):
    return [2.0 ** (-8.0 * (i + 1) / n) for i in range(n)]


def _sigmoid(v):
    return 1.0 / (1.0 + jnp.exp(-v))


def _mod_kernel(c_ref, w_ref, b_ref, lv_ref, mod_ref, lam_ref):
    c = c_ref[...]
    ca = c * _sigmoid(c)
    ca_hi = ca.astype(BF16)
    ca_lo = (ca - ca_hi.astype(F32)).astype(BF16)
    w = w_ref[...]
    w_hi = w.astype(BF16)
    w_lo = (w - w_hi.astype(F32)).astype(BF16)
    acc = jnp.dot(ca_hi, w_hi, preferred_element_type=F32)
    acc += jnp.dot(ca_lo, w_hi, preferred_element_type=F32)
    acc += jnp.dot(ca_hi, w_lo, preferred_element_type=F32)
    mod_ref[...] = acc + b_ref[...]
    lv = lv_ref[...]
    s1 = jnp.sum(lv[0:1, :] * lv[1:2, :], axis=1, keepdims=True)
    s2 = jnp.sum(lv[2:3, :] * lv[3:4, :], axis=1, keepdims=True)
    lam = jnp.exp(s1) - jnp.exp(s2) + LAM_INIT
    lam_ref[...] = jnp.broadcast_to(lam, lam_ref.shape)


def _mod_call(c8, w_ada, b_ada, lvec):
    d = c8.shape[1]
    n = w_ada.shape[1]
    tn = 512
    return pl.pallas_call(
        _mod_kernel,
        grid=(n // tn,),
        in_specs=[
            pl.BlockSpec((8, d), lambda j: (0, 0)),
            pl.BlockSpec((d, tn), lambda j: (0, j)),
            pl.BlockSpec((1, tn), lambda j: (0, j)),
            pl.BlockSpec((8, LANES), lambda j: (0, 0)),
        ],
        out_specs=[
            pl.BlockSpec((8, tn), lambda j: (0, j)),
            pl.BlockSpec((8, LANES), lambda j: (0, 0)),
        ],
        out_shape=[
            jax.ShapeDtypeStruct((8, n), F32),
            jax.ShapeDtypeStruct((8, LANES), F32),
        ],
        compiler_params=pltpu.CompilerParams(dimension_semantics=("arbitrary",)),
        name="adaln_mod",
    )(c8, w_ada, b_ada, lvec)


def _proj_kernel(x_ref, mod_ref, g_ref, wa_ref, wbt_ref,
                 km_ref, kd_ref, z_ref, qm_ref, qd_ref, vm_ref, vd_ref, kmean_ref):
    x = x_ref[0]
    ms = jnp.mean(x * x, axis=-1, keepdims=True)
    y = x * lax.rsqrt(ms + EPS) * g_ref[...]
    shift = mod_ref[0, 0:1, :]
    scl = mod_ref[0, 1:2, :]
    h = (y * (1.0 + scl) + shift).astype(BF16)

    pa = jnp.dot(h, wa_ref[...], preferred_element_type=F32)
    kw = MOBA_HEADS * HEAD_DIM
    for p in range(kw // LANES):
        km_ref[0, p, 0] = pa[:, p * LANES:(p + 1) * LANES].astype(BF16)
        kd_ref[0, p, 0] = pa[:, kw + p * LANES: kw + (p + 1) * LANES].astype(BF16)
    z_ref[0] = pa[:, 2 * kw:].astype(BF16)
    kmean_ref[0, 0] = jnp.mean(pa[:, :kw], axis=0, keepdims=True)

    pb = lax.dot_general(wbt_ref[...], h, (((1,), (1,)), ((), ())), preferred_element_type=F32)
    qm_ref[0, 0] = pb[0 * kw:1 * kw].astype(BF16)
    qd_ref[0, 0] = pb[1 * kw:2 * kw].astype(BF16)
    vm_ref[0, 0] = pb[2 * kw:3 * kw].astype(BF16)
    vd_ref[0, 0] = pb[3 * kw:4 * kw].astype(BF16)


def _proj_call(x, mod3, ln_g, wa, wbt):
    b, s, d = x.shape
    nb = s // BLK
    kw = MOBA_HEADS * HEAD_DIM
    nsl = kw // LANES
    kslab = jax.ShapeDtypeStruct((b, nsl, nb, BLK, LANES), BF16)
    tslab = jax.ShapeDtypeStruct((b, nb, kw, BLK), BF16)
    kspec = pl.BlockSpec((1, nsl, 1, BLK, LANES), lambda i, t: (i, 0, t, 0, 0))
    tspec = pl.BlockSpec((1, 1, kw, BLK), lambda i, t: (i, t, 0, 0))
    return pl.pallas_call(
        _proj_kernel,
        grid=(b, nb),
        in_specs=[
            pl.BlockSpec((1, BLK, d), lambda i, t: (i, t, 0)),
            pl.BlockSpec((1, 3, d), lambda i, t: (i, 0, 0)),
            pl.BlockSpec((1, d), lambda i, t: (0, 0)),
            pl.BlockSpec(wa.shape, lambda i, t: (0, 0)),
            pl.BlockSpec(wbt.shape, lambda i, t: (0, 0)),
        ],
        out_specs=[
            kspec, kspec,
            pl.BlockSpec((1, BLK, d), lambda i, t: (i, t, 0)),
            tspec, tspec, tspec, tspec,
            pl.BlockSpec((1, 1, 1, kw), lambda i, t: (i, t, 0, 0)),
        ],
        out_shape=[
            kslab, kslab,
            jax.ShapeDtypeStruct((b, s, d), BF16),
            tslab, tslab, tslab, tslab,
            jax.ShapeDtypeStruct((b, nb, 1, kw), F32),
        ],
        compiler_params=pltpu.CompilerParams(
            dimension_semantics=("arbitrary", "arbitrary"), vmem_limit_bytes=VMEM_LIMIT),
        name="norm_in_proj",
    )(x, mod3, ln_g, wa, wbt)


SEL_BLOCKS = 8


def _sel_kernel(km_ref, q_ref, sb_ref, *, sel_blocks):
    t = pl.program_id(2)
    km = km_ref[0, 0]
    km_hi = km.astype(BF16)
    km_lo = (km - km_hi.astype(F32)).astype(BF16)
    nb = km.shape[0]
    blk = lax.broadcasted_iota(jnp.int32, (nb, BLK), 0)
    for u in range(sel_blocks):
        q = q_ref[0, u]
        gate = (jnp.dot(km_hi, q, preferred_element_type=F32)
                + jnp.dot(km_lo, q, preferred_element_type=F32))
        valid = blk < t * sel_blocks + u
        g = jnp.where(valid, gate, -jnp.inf)
        sel = jnp.zeros(gate.shape, jnp.bool_)
        for _ in range(MOBA_TOPK):
            mx = jnp.max(g, axis=0, keepdims=True)
            first = jnp.min(jnp.where(g == mx, blk, nb), axis=0, keepdims=True)
            hit = blk == first
            sel = jnp.logical_or(sel, hit)
            g = jnp.where(hit, -jnp.inf, g)
        sb_ref[0, 0, :, u * BLK:(u + 1) * BLK] = jnp.where(jnp.logical_and(sel, valid), 0.0, NEG)


def _sel_call(kmean_h, qt_m):
    b, nh, nb, hd = kmean_h.shape
    sel_blocks = SEL_BLOCKS if nb % SEL_BLOCKS == 0 else 1
    return pl.pallas_call(
        functools.partial(_sel_kernel, sel_blocks=sel_blocks),
        grid=(b, nh, nb // sel_blocks),
        in_specs=[
            pl.BlockSpec((1, 1, nb, hd), lambda i, h, t: (i, h, 0, 0)),
            pl.BlockSpec((1, sel_blocks, hd, BLK), lambda i, h, t: (i, t, h, 0)),
        ],
        out_specs=pl.BlockSpec((1, 1, nb, sel_blocks * BLK), lambda i, h, t: (i, h, 0, t)),
        out_shape=jax.ShapeDtypeStruct((b, nh, nb, nb * BLK), F32),
        compiler_params=pltpu.CompilerParams(dimension_semantics=("arbitrary",) * 3),
        name="moba_select",
    )(kmean_h, qt_m)


def _one_rows(first_row):
    r = lax.broadcasted_iota(jnp.int32, (LANES, BLK), 0)
    return jnp.where(jnp.logical_and(r >= first_row, r < first_row + 2), 1.0, 0.0).astype(BF16)


M_INIT = -1e20
NCOL = 2 * BLK


def _flash(qi, nb, scores, pv, bias, acc_refs, acc_cols, s_buf, p_buf):
    def soft(s, b, m, l):
        cm = jnp.max(s, axis=0, keepdims=True)
        if b is not None:
            cm = cm + b
        m_new = jnp.maximum(m, cm)
        alpha = jnp.exp(m - m_new)
        p = jnp.exp(s + ((-m_new) if b is None else (b - m_new)))
        l = alpha * l + jnp.sum(p, axis=0, keepdims=True)
        return m_new, l, alpha, p.astype(BF16)

    def accumulate(alpha, parts):
        for ref, d, c in zip(acc_refs, parts, acc_cols):
            ref[...] = alpha[:, c] * ref[...] + d

    def visible(blk):
        r = lax.broadcasted_iota(jnp.int32, (BLK, NCOL), 0)
        c = lax.broadcasted_iota(jnp.int32, (BLK, NCOL), 1)
        return r + (blk - qi) * BLK <= jnp.bitwise_and(c, BLK - 1)

    def pair(i, carry, last):
        m, l, a_prev = carry
        a = 2 * i
        b = a + 1
        s_buf[1] = scores(jnp.minimum(b, nb - 1))
        accumulate(a_prev, pv(jnp.maximum(a - 1, 0), p_buf[1]))
        s = s_buf[0]
        ba = None if bias is None else bias(a)
        if last:
            s = jnp.where(visible(a), s, NEG)
            if ba is not None:
                ba = jnp.where(a < qi, ba, 0.0)
        m, l, alpha, p = soft(s, ba, m, l)
        p_buf[0] = p
        if not last:
            s_buf[0] = scores(b + 1)
        accumulate(alpha, pv(a, p_buf[0]))
        s = s_buf[1]
        bb = None if (bias is None or last) else bias(b)
        if last:
            s = jnp.where(visible(b), s, NEG)
        m, l, alpha, p = soft(s, bb, m, l)
        p_buf[1] = p
        return m, l, alpha

    for ref in acc_refs:
        ref[...] = jnp.zeros(ref.shape, F32)
    s_buf[0] = scores(0)
    p_buf[1] = jnp.zeros((BLK, NCOL), BF16)
    carry = (jnp.full((1, NCOL), M_INIT, F32), jnp.zeros((1, NCOL), F32), jnp.ones((1, NCOL), F32))
    npair = qi // 2
    carry = lax.fori_loop(0, npair, lambda i, c: pair(i, c, False), carry)
    m, l, alpha = pair(npair, carry, True)
    accumulate(alpha, pv(jnp.minimum(2 * npair + 1, nb - 1), p_buf[1]))
    return l


def _moba_kernel(q_ref, k_ref, aug_ref, v_ref, sb_ref, o_ref, s_buf, p_buf, acc0_ref, acc1_ref):
    qi = pl.program_id(2)
    q = q_ref[0, 0]
    zero = jnp.zeros((HEAD_DIM, BLK), BF16)
    qp = jnp.concatenate([
        jnp.concatenate([q[0:HEAD_DIM], zero, _one_rows(0)], axis=0),
        jnp.concatenate([zero, q[HEAD_DIM:], _one_rows(2)], axis=0)], axis=1)

    def scores(j):
        kd = jnp.concatenate([k_ref[0, 0, j], aug_ref[0, j]], axis=1)
        return jnp.dot(kd, qp, preferred_element_type=F32)

    def pv(j, pb):
        v = v_ref[0, j]
        return (jnp.dot(v[0:HEAD_DIM], pb[:, :BLK], preferred_element_type=F32),
                jnp.dot(v[HEAD_DIM:], pb[:, BLK:], preferred_element_type=F32))

    def bias(j):
        return jnp.concatenate([sb_ref[0, 0, pl.ds(j, 1), :], sb_ref[0, 1, pl.ds(j, 1), :]], axis=1)

    halves = (slice(0, BLK), slice(BLK, NCOL))
    l = _flash(qi, k_ref.shape[2], scores, pv, bias, (acc0_ref, acc1_ref), halves, s_buf, p_buf)
    out = jnp.concatenate([acc0_ref[...] / l[:, :BLK], acc1_ref[...] / l[:, BLK:]], axis=0)
    o_ref[0] = out.T.astype(BF16)


def _moba_call(qt_m, k_m, aug_m, vt_m, selbias):
    b, nb, kw, _ = qt_m.shape
    npair = kw // LANES
    s = nb * BLK
    return pl.pallas_call(
        _moba_kernel,
        grid=(b, npair, nb),
        in_specs=[
            pl.BlockSpec((1, 1, LANES, BLK), lambda i, p, t: (i, t, p, 0)),
            pl.BlockSpec((1, 1, nb, BLK, LANES), lambda i, p, t: (i, p, 0, 0, 0)),
            pl.BlockSpec((1, nb, BLK, LANES), lambda i, p, t: (p, 0, 0, 0)),
            pl.BlockSpec((1, nb, LANES, BLK), lambda i, p, t: (i, 0, p, 0)),
            pl.BlockSpec((1, 2, nb, BLK), lambda i, p, t: (i, p, 0, t)),
        ],
        out_specs=pl.BlockSpec((1, BLK, LANES), lambda i, p, t: (i, t, p)),
        out_shape=jax.ShapeDtypeStruct((b, s, kw), BF16),
        scratch_shapes=[
            pltpu.VMEM((2, BLK, NCOL), F32),
            pltpu.VMEM((2, BLK, NCOL), BF16),
            pltpu.VMEM((HEAD_DIM, BLK), F32),
            pltpu.VMEM((HEAD_DIM, BLK), F32),
        ],
        compiler_params=pltpu.CompilerParams(
            dimension_semantics=("arbitrary",) * 3, vmem_limit_bytes=VMEM_LIMIT),
        name="moba_attn",
    )(qt_m, k_m, aug_m, vt_m, selbias)


def _diff_kernel(lam_ref, q_ref, k_ref, aug_ref, v_ref, g_ref, o_ref, s_buf, p_buf, acc_ref):
    qi = pl.program_id(2)
    q = q_ref[0, 0]
    zero = jnp.zeros((HEAD_DIM, BLK), BF16)
    ones = _one_rows(0)
    qp = jnp.concatenate([
        jnp.concatenate([q[0:HEAD_DIM], zero, ones], axis=0),
        jnp.concatenate([zero, q[HEAD_DIM:], ones], axis=0)], axis=1)

    def scores(j):
        kd = jnp.concatenate([k_ref[0, 0, j], aug_ref[0, j]], axis=1)
        return jnp.dot(kd, qp, preferred_element_type=F32)

    def pv(j, pb):
        return (jnp.dot(v_ref[0, j], pb, preferred_element_type=F32),)

    l = _flash(qi, k_ref.shape[2], scores, pv, None, (acc_ref,), (slice(0, NCOL),), s_buf, p_buf)
    o = acc_ref[...] / l
    d = o[:, :BLK] - lam_ref[0, 0] * o[:, BLK:]
    ms = jnp.mean(d * d, axis=0, keepdims=True)
    y = d * lax.rsqrt(ms + EPS) * g_ref[...] * (1.0 - LAM_INIT)
    o_ref[0] = y.T.astype(BF16)


def _diff_call(lam, qt_d, k_d, aug_d, vt_d, subln_col):
    b, nb, kw, _ = qt_d.shape
    nh = kw // LANES
    s = nb * BLK
    return pl.pallas_call(
        _diff_kernel,
        grid=(b, nh, nb),
        in_specs=[
            pl.BlockSpec(memory_space=pltpu.SMEM),
            pl.BlockSpec((1, 1, LANES, BLK), lambda i, h, t: (i, t, h, 0)),
            pl.BlockSpec((1, 1, nb, BLK, LANES), lambda i, h, t: (i, h, 0, 0, 0)),
            pl.BlockSpec((1, nb, BLK, LANES), lambda i, h, t: (h, 0, 0, 0)),
            pl.BlockSpec((1, nb, LANES, BLK), lambda i, h, t: (i, 0, h, 0)),
            pl.BlockSpec((LANES, 1), lambda i, h, t: (0, 0)),
        ],
        out_specs=pl.BlockSpec((1, BLK, LANES), lambda i, h, t: (i, t, h)),
        out_shape=jax.ShapeDtypeStruct((b, s, kw), BF16),
        scratch_shapes=[
            pltpu.VMEM((2, BLK, NCOL), F32),
            pltpu.VMEM((2, BLK, NCOL), BF16),
            pltpu.VMEM((2 * HEAD_DIM, NCOL), F32),
        ],
        compiler_params=pltpu.CompilerParams(
            dimension_semantics=("arbitrary",) * 3, vmem_limit_bytes=VMEM_LIMIT),
        name="diff_attn",
    )(lam, qt_d, k_d, aug_d, vt_d, subln_col)


def _out_kernel(a_ref, b_ref, z_ref, x_ref, mod_ref, w_ref, g_ref, o_ref):
    z = z_ref[0].astype(F32)
    heads = jnp.concatenate([a_ref[0], b_ref[0]], axis=-1).astype(F32)
    y = (heads * (z * _sigmoid(z))).astype(BF16)
    r = jnp.dot(y, w_ref[...], preferred_element_type=F32)
    xo = x_ref[0] + mod_ref[0, 2:3, :] * r
    ms = jnp.mean(xo * xo, axis=-1, keepdims=True)
    o_ref[0] = xo * lax.rsqrt(ms + EPS) * g_ref[...]


def _out_call(a_out, b_out, z, x, mod3, w_out, final_g):
    b, s, d = x.shape
    tm = 512
    kw = a_out.shape[-1]
    return pl.pallas_call(
        _out_kernel,
        grid=(b, s // tm),
        in_specs=[
            pl.BlockSpec((1, tm, kw), lambda i, t: (i, t, 0)),
            pl.BlockSpec((1, tm, kw), lambda i, t: (i, t, 0)),
            pl.BlockSpec((1, tm, d), lambda i, t: (i, t, 0)),
            pl.BlockSpec((1, tm, d), lambda i, t: (i, t, 0)),
            pl.BlockSpec((1, 3, d), lambda i, t: (i, 0, 0)),
            pl.BlockSpec(w_out.shape, lambda i, t: (0, 0)),
            pl.BlockSpec((1, d), lambda i, t: (0, 0)),
        ],
        out_specs=pl.BlockSpec((1, tm, d), lambda i, t: (i, t, 0)),
        out_shape=jax.ShapeDtypeStruct((b, s, d), F32),
        compiler_params=pltpu.CompilerParams(
            dimension_semantics=("arbitrary", "arbitrary"), vmem_limit_bytes=VMEM_LIMIT),
        name="gate_out_proj_norm",
    )(a_out, b_out, z, x, mod3, w_out, final_g)


def _alibi_columns(slopes_per_slab, s):
    pos = np.arange(s)
    hi = (pos // POS_SPLIT) * POS_SPLIT
    lo = pos % POS_SPLIT
    out = np.zeros((len(slopes_per_slab), s, LANES), np.float32)
    for n, slopes in enumerate(slopes_per_slab):
        for i, m in enumerate(slopes):
            out[n, :, 2 * i] = m * hi
            out[n, :, 2 * i + 1] = m * lo
    return jnp.asarray(out.reshape(len(slopes_per_slab), s // BLK, BLK, LANES), dtype=BF16)


def _forward(x, c, ln_g, w_ada, b_ada, w_in, lambda_q1, lambda_k1, lambda_q2, lambda_k2, subln_g, w_out, final_g):
    b, s, d = x.shape
    kw = MOBA_HEADS * HEAD_DIM
    assert s % BLK == 0 and d == 2 * kw and w_in.shape[-1] == 6 * kw + d

    c8 = jnp.pad(c, ((0, 8 - b), (0, 0)))
    lvec = jnp.pad(jnp.stack([lambda_q1[0], lambda_k1[0], lambda_q2[0], lambda_k2[0]]),
                   ((0, 4), (0, LANES - HEAD_DIM)))
    mod, lam = _mod_call(c8, w_ada[0], b_ada[0][None, :], lvec)
    mod3 = mod[:b].reshape(b, 3, d)
    lam = lam[0:1, 0:1]

    w = w_in[0]
    mq, mk, mv, dq, dk, dv = (w[:, i * kw:(i + 1) * kw] for i in range(6))
    wz = w[:, 6 * kw:]
    scale = HEAD_DIM ** -0.5
    wa = jnp.concatenate([mk, dk, wz], axis=1).astype(BF16)
    wbt = jnp.concatenate([mq * scale, dq * scale, mv, dv], axis=1).T.astype(BF16)

    k_m, k_d, z, qt_m, qt_d, vt_m, vt_d, kmean = _proj_call(x, mod3, ln_g[0][None, :], wa, wbt)

    nb = s // BLK
    kmean_h = kmean.reshape(b, nb, MOBA_HEADS, HEAD_DIM).transpose(0, 2, 1, 3)
    selbias = _sel_call(kmean_h, qt_m)

    ms = _alibi_slopes(MOBA_HEADS)
    aug_m = _alibi_columns([ms[2 * p:2 * p + 2] for p in range(MOBA_HEADS // 2)], s)
    aug_d = _alibi_columns([[m] for m in _alibi_slopes(DIFF_HEADS)], s)

    a_out = _moba_call(qt_m, k_m, aug_m, vt_m, selbias)
    b_out = _diff_call(lam, qt_d, k_d, aug_d, vt_d, subln_g[0][:, None])
    out = _out_call(a_out, b_out, z, x, mod3, w_out[0].astype(BF16), final_g[None, :])
    return out, a_out, b_out


def kernel(x, c, ln_g, w_ada, b_ada, w_in, lambda_q1, lambda_k1, lambda_q2, lambda_k2, subln_g, w_out, final_g):
    return _forward(x, c, ln_g, w_ada, b_ada, w_in, lambda_q1, lambda_k1, lambda_q2, lambda_k2,
                    subln_g, w_out, final_g)[0]
```

```python
import functools

import numpy as np
import jax
import jax.numpy as jnp
from jax import lax
from jax.experimental import pallas as pl
from jax.experimental.pallas import tpu as pltpu

F32 = jnp.float32
BF16 = jnp.bfloat16

HEAD_DIM = 64
MOBA_HEADS = 8
DIFF_HEADS = 4
MOBA_BLOCK = 256
MOBA_TOPK = 3
EPS = 1e-6
LAM_INIT = 0.8 - 0.6 * float(np.exp(-0.3 * 0))

BLK = MOBA_BLOCK
TQ = 2 * BLK
LANES = 128
POS_SPLIT = 64
NEG = -1e30
VMEM_LIMIT = 48 * 1024 * 1024


def _alibi_slopes(n):
    return [2.0 ** (-8.0 * (i + 1) / n) for i in range(n)]


def _sigmoid(v):
    return 1.0 / (1.0 + jnp.exp(-v))


def _mod_kernel(c_ref, w_ref, b_ref, lv_ref, mod_ref, lam_ref):
    c = c_ref[...]
    ca = c * _sigmoid(c)
    ca_hi = ca.astype(BF16)
    ca_lo = (ca - ca_hi.astype(F32)).astype(BF16)
    w = w_ref[...]
    w_hi = w.astype(BF16)
    w_lo = (w - w_hi.astype(F32)).astype(BF16)
    acc = jnp.dot(ca_hi, w_hi, preferred_element_type=F32)
    acc += jnp.dot(ca_lo, w_hi, preferred_element_type=F32)
    acc += jnp.dot(ca_hi, w_lo, preferred_element_type=F32)
    mod_ref[...] = acc + b_ref[...]
    lv = lv_ref[...]
    s1 = jnp.sum(lv[0:1, :] * lv[1:2, :], axis=1, keepdims=True)
    s2 = jnp.sum(lv[2:3, :] * lv[3:4, :], axis=1, keepdims=True)
    lam = jnp.exp(s1) - jnp.exp(s2) + LAM_INIT
    lam_ref[...] = jnp.broadcast_to(lam, lam_ref.shape)


def _mod_call(c8, w_ada, b_ada, lvec):
    d = c8.shape[1]
    n = w_ada.shape[1]
    tn = 512
    return pl.pallas_call(
        _mod_kernel,
        grid=(n // tn,),
        in_specs=[
            pl.BlockSpec((8, d), lambda j: (0, 0)),
            pl.BlockSpec((d, tn), lambda j: (0, j)),
            pl.BlockSpec((1, tn), lambda j: (0, j)),
            pl.BlockSpec((8, LANES), lambda j: (0, 0)),
        ],
        out_specs=[
            pl.BlockSpec((8, tn), lambda j: (0, j)),
            pl.BlockSpec((8, LANES), lambda j: (0, 0)),
        ],
        out_shape=[
            jax.ShapeDtypeStruct((8, n), F32),
            jax.ShapeDtypeStruct((8, LANES), F32),
        ],
        compiler_params=pltpu.CompilerParams(dimension_semantics=("arbitrary",)),
        name="adaln_mod",
    )(c8, w_ada, b_ada, lvec)


def _proj_kernel(x_ref, mod_ref, g_ref, wa_ref, wbt_ref,
                 km_ref, kd_ref, z_ref, qm_ref, qd_ref, vm_ref, vd_ref, kmean_ref):
    x = x_ref[0]
    ms = jnp.mean(x * x, axis=-1, keepdims=True)
    y = x * lax.rsqrt(ms + EPS) * g_ref[...]
    shift = mod_ref[0, 0:1, :]
    scl = mod_ref[0, 1:2, :]
    h = (y * (1.0 + scl) + shift).astype(BF16)

    pa = jnp.dot(h, wa_ref[...], preferred_element_type=F32)
    kw = MOBA_HEADS * HEAD_DIM
    for p in range(kw // LANES):
        km_ref[0, p, 0] = pa[:, p * LANES:(p + 1) * LANES].astype(BF16)
        kd_ref[0, p, 0] = pa[:, kw + p * LANES: kw + (p + 1) * LANES].astype(BF16)
    z_ref[0] = pa[:, 2 * kw:].astype(BF16)
    kmean_ref[0, 0] = jnp.mean(pa[:, :kw], axis=0, keepdims=True)

    pb = lax.dot_general(wbt_ref[...], h, (((1,), (1,)), ((), ())), preferred_element_type=F32)
    qm_ref[0, 0] = pb[0 * kw:1 * kw].astype(BF16)
    qd_ref[0, 0] = pb[1 * kw:2 * kw].astype(BF16)
    vm_ref[0, 0] = pb[2 * kw:3 * kw].astype(BF16)
    vd_ref[0, 0] = pb[3 * kw:4 * kw].astype(BF16)


def _proj_call(x, mod3, ln_g, wa, wbt):
    b, s, d = x.shape
    nb = s // BLK
    kw = MOBA_HEADS * HEAD_DIM
    nsl = kw // LANES
    kslab = jax.ShapeDtypeStruct((b, nsl, nb, BLK, LANES), BF16)
    tslab = jax.ShapeDtypeStruct((b, nb, kw, BLK), BF16)
    kspec = pl.BlockSpec((1, nsl, 1, BLK, LANES), lambda i, t: (i, 0, t, 0, 0))
    tspec = pl.BlockSpec((1, 1, kw, BLK), lambda i, t: (i, t, 0, 0))
    return pl.pallas_call(
        _proj_kernel,
        grid=(b, nb),
        in_specs=[
            pl.BlockSpec((1, BLK, d), lambda i, t: (i, t, 0)),
            pl.BlockSpec((1, 3, d), lambda i, t: (i, 0, 0)),
            pl.BlockSpec((1, d), lambda i, t: (0, 0)),
            pl.BlockSpec(wa.shape, lambda i, t: (0, 0)),
            pl.BlockSpec(wbt.shape, lambda i, t: (0, 0)),
        ],
        out_specs=[
            kspec, kspec,
            pl.BlockSpec((1, BLK, d), lambda i, t: (i, t, 0)),
            tspec, tspec, tspec, tspec,
            pl.BlockSpec((1, 1, 1, kw), lambda i, t: (i, t, 0, 0)),
        ],
        out_shape=[
            kslab, kslab,
            jax.ShapeDtypeStruct((b, s, d), BF16),
            tslab, tslab, tslab, tslab,
            jax.ShapeDtypeStruct((b, nb, 1, kw), F32),
        ],
        compiler_params=pltpu.CompilerParams(
            dimension_semantics=("arbitrary", "arbitrary"), vmem_limit_bytes=VMEM_LIMIT),
        name="norm_in_proj",
    )(x, mod3, ln_g, wa, wbt)


SEL_BLOCKS = 8


def _sel_kernel(km_ref, q_ref, sb_ref, *, sel_blocks):
    t = pl.program_id(2)
    km = km_ref[0, 0]
    km_hi = km.astype(BF16)
    km_lo = (km - km_hi.astype(F32)).astype(BF16)
    nb = km.shape[0]
    blk = lax.broadcasted_iota(jnp.int32, (nb, BLK), 0)
    for u in range(sel_blocks):
        q = q_ref[0, u]
        gate = (jnp.dot(km_hi, q, preferred_element_type=F32)
                + jnp.dot(km_lo, q, preferred_element_type=F32))
        valid = blk < t * sel_blocks + u
        g = jnp.where(valid, gate, -jnp.inf)
        sel = jnp.zeros(gate.shape, jnp.bool_)
        for _ in range(MOBA_TOPK):
            mx = jnp.max(g, axis=0, keepdims=True)
            first = jnp.min(jnp.where(g == mx, blk, nb), axis=0, keepdims=True)
            hit = blk == first
            sel = jnp.logical_or(sel, hit)
            g = jnp.where(hit, -jnp.inf, g)
        sb_ref[0, 0, :, u * BLK:(u + 1) * BLK] = jnp.where(jnp.logical_and(sel, valid), 0.0, NEG)


def _sel_call(kmean_h, qt_m):
    b, nh, nb, hd = kmean_h.shape
    sel_blocks = SEL_BLOCKS if nb % SEL_BLOCKS == 0 else 1
    return pl.pallas_call(
        functools.partial(_sel_kernel, sel_blocks=sel_blocks),
        grid=(b, nh, nb // sel_blocks),
        in_specs=[
            pl.BlockSpec((1, 1, nb, hd), lambda i, h, t: (i, h, 0, 0)),
            pl.BlockSpec((1, sel_blocks, hd, BLK), lambda i, h, t: (i, t, h, 0)),
        ],
        out_specs=pl.BlockSpec((1, 1, nb, sel_blocks * BLK), lambda i, h, t: (i, h, 0, t)),
        out_shape=jax.ShapeDtypeStruct((b, nh, nb, nb * BLK), F32),
        compiler_params=pltpu.CompilerParams(dimension_semantics=("arbitrary",) * 3),
        name="moba_select",
    )(kmean_h, qt_m)


def _bf16_split(value, parts):
    out, rest = [], np.float64(value)
    for _ in range(parts):
        piece = np.float32(rest).astype(BF16)
        out.append(float(piece))
        rest -= np.float64(piece)
    return out


LOG2E_PARTS = _bf16_split(np.log2(np.e), 3)
ALIBI_COLS = 2 * len(LOG2E_PARTS)
SUB = 16


def _alibi_rows(first_row):
    r = lax.broadcasted_iota(jnp.int32, (LANES, TQ), 0)
    out = jnp.zeros((LANES, TQ), F32)
    for i in range(ALIBI_COLS):
        out = jnp.where(r == first_row + i, LOG2E_PARTS[i % len(LOG2E_PARTS)], out)
    return out.astype(BF16)


def _ones_rows():
    r = lax.broadcasted_iota(jnp.int32, (SUB, BLK), 0)
    return jnp.where(r == 0, 1.0, 0.0).astype(BF16)


M_INIT = -1e20
NCOL = 2 * TQ


def _flash(qi, scores, pv, bias, acc_refs, acc_cols, s_buf, p_buf):
    def colmax(s):
        return jnp.max(s, axis=0, keepdims=True)

    def soft(s, cm, b, m):
        if b is not None:
            cm = cm + b
        m_new = jnp.maximum(m, cm)
        alpha = jnp.exp2(m - m_new)
        p = jnp.exp2(s + ((-m_new) if b is None else (b - m_new)))
        return m_new, alpha, p.astype(BF16)

    def accumulate(alpha1, parts1, alpha2, parts2):
        for ref, d1, d2, c in zip(acc_refs, parts1, parts2, acc_cols):
            ref[...] = alpha2[:, c] * (alpha1[:, c] * ref[...] + d1) + d2

    def visible(first_row):
        r = lax.broadcasted_iota(jnp.int32, (BLK, NCOL), 0)
        c = lax.broadcasted_iota(jnp.int32, (BLK, NCOL), 1)
        return r + first_row <= jnp.bitwise_and(c, TQ - 1)

    def pair(i, carry, last):
        m, alpha_p, alpha_q, cm_a = carry
        a = 2 * i
        b = a + 1
        accumulate(alpha_p, pv(jnp.maximum(a - 2, 0), p_buf[0]), alpha_q, pv(jnp.maximum(a - 1, 0), p_buf[1]))
        sb = scores(b)
        if last:
            sb = jnp.where(visible(BLK), sb, NEG)
        s_buf[1] = sb
        cm_b = colmax(sb)
        sa = s_buf[0]
        ba = None if bias is None else bias(a)
        if last:
            sa = jnp.where(visible(0), sa, NEG)
            cm_a = colmax(sa)
            if ba is not None:
                cq = jnp.bitwise_and(lax.broadcasted_iota(jnp.int32, (1, NCOL), 1), TQ - 1)
                ba = jnp.where(cq < BLK, 0.0, ba)
        m, alpha_a, p = soft(sa, cm_a, ba, m)
        p_buf[0] = p
        cm_n = cm_b
        if not last:
            sn = scores(b + 1)
            s_buf[0] = sn
            cm_n = colmax(sn)
        bb = None if (bias is None or last) else bias(b)
        m, alpha_b, p = soft(s_buf[1], cm_b, bb, m)
        p_buf[1] = p
        return m, alpha_a, alpha_b, cm_n

    for ref in acc_refs:
        ref[...] = jnp.zeros(ref.shape, F32)
    s0 = scores(0)
    s_buf[0] = s0
    p_buf[...] = jnp.zeros(p_buf.shape, BF16)
    one = jnp.ones((1, NCOL), F32)
    carry = (jnp.full((1, NCOL), M_INIT, F32), one, one, colmax(s0))
    carry = lax.fori_loop(0, qi // 2, lambda i, c: pair(2 * i + 1, pair(2 * i, c, False), False), carry)
    carry = lax.fori_loop(0, qi % 2, lambda _, c: pair(qi - 1, c, False), carry)
    _, alpha_a, alpha_b, _ = pair(qi, carry, True)
    accumulate(alpha_a, pv(2 * qi, p_buf[0]), alpha_b, pv(2 * qi + 1, p_buf[1]))


def _moba_kernel(q_ref, k_ref, aug_ref, v_ref, sb_ref, o_ref, s_buf, p_buf, acc0_ref, acc1_ref):
    qi = pl.program_id(2)
    q = jnp.concatenate([q_ref[0, 0], q_ref[0, 1]], axis=1)
    zero = jnp.zeros((HEAD_DIM, TQ), BF16)
    qp = jnp.concatenate([
        jnp.concatenate([q[0:HEAD_DIM], zero, _alibi_rows(0)], axis=0),
        jnp.concatenate([zero, q[HEAD_DIM:], _alibi_rows(ALIBI_COLS)], axis=0)], axis=1)
    ones = _ones_rows()

    def scores(j):
        kd = jnp.concatenate([k_ref[0, 0, j], aug_ref[0, j]], axis=1)
        return jnp.dot(kd, qp, preferred_element_type=F32)

    def pv(j, pb):
        v = v_ref[0, j]
        va = jnp.concatenate([v[0:HEAD_DIM], ones, v[HEAD_DIM:]], axis=0)
        return (jnp.dot(va[0:HEAD_DIM + SUB], pb[:, :TQ], preferred_element_type=F32),
                jnp.dot(va[HEAD_DIM:], pb[:, TQ:], preferred_element_type=F32))

    def bias(j):
        return jnp.concatenate([sb_ref[0, 0, pl.ds(j, 1), :], sb_ref[0, 1, pl.ds(j, 1), :]], axis=1)

    halves = (slice(0, TQ), slice(TQ, NCOL))
    _flash(qi, scores, pv, bias, (acc0_ref, acc1_ref), halves, s_buf, p_buf)
    acc0 = acc0_ref[...]
    acc1 = acc1_ref[...]
    out = jnp.concatenate([acc0[0:HEAD_DIM] / acc0[HEAD_DIM:HEAD_DIM + 1],
                           acc1[SUB:] / acc1[0:1]], axis=0)
    o_ref[0] = out.T.astype(BF16)


def _moba_call(qt_m, k_m, aug_m, vt_m, selbias):
    b, nb, kw, _ = qt_m.shape
    npair = kw // LANES
    s = nb * BLK
    return pl.pallas_call(
        _moba_kernel,
        grid=(b, npair, s // TQ),
        in_specs=[
            pl.BlockSpec((1, TQ // BLK, LANES, BLK), lambda i, p, t: (i, t, p, 0)),
            pl.BlockSpec((1, 1, nb, BLK, LANES), lambda i, p, t: (i, p, 0, 0, 0)),
            pl.BlockSpec((1, nb, BLK, LANES), lambda i, p, t: (p, 0, 0, 0)),
            pl.BlockSpec((1, nb, LANES, BLK), lambda i, p, t: (i, 0, p, 0)),
            pl.BlockSpec((1, 2, nb, TQ), lambda i, p, t: (i, p, 0, t)),
        ],
        out_specs=pl.BlockSpec((1, TQ, LANES), lambda i, p, t: (i, t, p)),
        out_shape=jax.ShapeDtypeStruct((b, s, kw), BF16),
        scratch_shapes=[
            pltpu.VMEM((2, BLK, NCOL), F32),
            pltpu.VMEM((2, BLK, NCOL), BF16),
            pltpu.VMEM((HEAD_DIM + SUB, TQ), F32),
            pltpu.VMEM((HEAD_DIM + SUB, TQ), F32),
        ],
        compiler_params=pltpu.CompilerParams(
            dimension_semantics=("arbitrary",) * 3, vmem_limit_bytes=VMEM_LIMIT),
        name="moba_attn",
    )(qt_m, k_m, aug_m, vt_m, selbias)


def _diff_kernel(lam_ref, q_ref, k_ref, aug_ref, v_ref, g_ref, o_ref, s_buf, p_buf, acc_ref):
    qi = pl.program_id(2)
    q = jnp.concatenate([q_ref[0, 0], q_ref[0, 1]], axis=1)
    zero = jnp.zeros((HEAD_DIM, TQ), BF16)
    arows = _alibi_rows(0)
    qp = jnp.concatenate([
        jnp.concatenate([q[0:HEAD_DIM], zero, arows], axis=0),
        jnp.concatenate([zero, q[HEAD_DIM:], arows], axis=0)], axis=1)
    ones = _ones_rows()

    def scores(j):
        kd = jnp.concatenate([k_ref[0, 0, j], aug_ref[0, j]], axis=1)
        return jnp.dot(kd, qp, preferred_element_type=F32)

    def pv(j, pb):
        va = jnp.concatenate([v_ref[0, j], ones], axis=0)
        return (jnp.dot(va, pb, preferred_element_type=F32),)

    _flash(qi, scores, pv, None, (acc_ref,), (slice(0, NCOL),), s_buf, p_buf)
    acc = acc_ref[...]
    o = acc[0:2 * HEAD_DIM] / acc[2 * HEAD_DIM:2 * HEAD_DIM + 1]
    d = o[:, :TQ] - lam_ref[0, 0] * o[:, TQ:]
    ms = jnp.mean(d * d, axis=0, keepdims=True)
    y = d * lax.rsqrt(ms + EPS) * g_ref[...] * (1.0 - LAM_INIT)
    o_ref[0] = y.T.astype(BF16)


def _diff_call(lam, qt_d, k_d, aug_d, vt_d, subln_col):
    b, nb, kw, _ = qt_d.shape
    nh = kw // LANES
    s = nb * BLK
    return pl.pallas_call(
        _diff_kernel,
        grid=(b, nh, s // TQ),
        in_specs=[
            pl.BlockSpec(memory_space=pltpu.SMEM),
            pl.BlockSpec((1, TQ // BLK, LANES, BLK), lambda i, h, t: (i, t, h, 0)),
            pl.BlockSpec((1, 1, nb, BLK, LANES), lambda i, h, t: (i, h, 0, 0, 0)),
            pl.BlockSpec((1, nb, BLK, LANES), lambda i, h, t: (h, 0, 0, 0)),
            pl.BlockSpec((1, nb, LANES, BLK), lambda i, h, t: (i, 0, h, 0)),
            pl.BlockSpec((LANES, 1), lambda i, h, t: (0, 0)),
        ],
        out_specs=pl.BlockSpec((1, TQ, LANES), lambda i, h, t: (i, t, h)),
        out_shape=jax.ShapeDtypeStruct((b, s, kw), BF16),
        scratch_shapes=[
            pltpu.VMEM((2, BLK, NCOL), F32),
            pltpu.VMEM((2, BLK, NCOL), BF16),
            pltpu.VMEM((2 * HEAD_DIM + SUB, NCOL), F32),
        ],
        compiler_params=pltpu.CompilerParams(
            dimension_semantics=("arbitrary",) * 3, vmem_limit_bytes=VMEM_LIMIT),
        name="diff_attn",
    )(lam, qt_d, k_d, aug_d, vt_d, subln_col)


def _out_kernel(a_ref, b_ref, z_ref, x_ref, mod_ref, w_ref, g_ref, o_ref):
    z = z_ref[0].astype(F32)
    heads = jnp.concatenate([a_ref[0], b_ref[0]], axis=-1).astype(F32)
    y = (heads * (z * _sigmoid(z))).astype(BF16)
    r = jnp.dot(y, w_ref[...], preferred_element_type=F32)
    xo = x_ref[0] + mod_ref[0, 2:3, :] * r
    ms = jnp.mean(xo * xo, axis=-1, keepdims=True)
    o_ref[0] = xo * lax.rsqrt(ms + EPS) * g_ref[...]


def _out_call(a_out, b_out, z, x, mod3, w_out, final_g):
    b, s, d = x.shape
    tm = 512
    kw = a_out.shape[-1]
    return pl.pallas_call(
        _out_kernel,
        grid=(b, s // tm),
        in_specs=[
            pl.BlockSpec((1, tm, kw), lambda i, t: (i, t, 0)),
            pl.BlockSpec((1, tm, kw), lambda i, t: (i, t, 0)),
            pl.BlockSpec((1, tm, d), lambda i, t: (i, t, 0)),
            pl.BlockSpec((1, tm, d), lambda i, t: (i, t, 0)),
            pl.BlockSpec((1, 3, d), lambda i, t: (i, 0, 0)),
            pl.BlockSpec(w_out.shape, lambda i, t: (0, 0)),
            pl.BlockSpec((1, d), lambda i, t: (0, 0)),
        ],
        out_specs=pl.BlockSpec((1, tm, d), lambda i, t: (i, t, 0)),
        out_shape=jax.ShapeDtypeStruct((b, s, d), F32),
        compiler_params=pltpu.CompilerParams(
            dimension_semantics=("arbitrary", "arbitrary"), vmem_limit_bytes=VMEM_LIMIT),
        name="gate_out_proj_norm",
    )(a_out, b_out, z, x, mod3, w_out, final_g)


def _alibi_columns(slopes_per_slab, s):
    pos = np.arange(s)
    hi = (pos // POS_SPLIT) * POS_SPLIT
    lo = pos % POS_SPLIT
    half = ALIBI_COLS // 2
    out = np.zeros((len(slopes_per_slab), s, LANES), np.float32)
    for n, slopes in enumerate(slopes_per_slab):
        for i, m in enumerate(slopes):
            out[n, :, ALIBI_COLS * i:ALIBI_COLS * i + half] = (m * hi)[:, None]
            out[n, :, ALIBI_COLS * i + half:ALIBI_COLS * (i + 1)] = (m * lo)[:, None]
    return jnp.asarray(out.reshape(len(slopes_per_slab), s // BLK, BLK, LANES), dtype=BF16)


def _forward(x, c, ln_g, w_ada, b_ada, w_in, lambda_q1, lambda_k1, lambda_q2, lambda_k2, subln_g, w_out, final_g):
    b, s, d = x.shape
    kw = MOBA_HEADS * HEAD_DIM
    assert s % TQ == 0 and d == 2 * kw and w_in.shape[-1] == 6 * kw + d

    c8 = jnp.pad(c, ((0, 8 - b), (0, 0)))
    lvec = jnp.pad(jnp.stack([lambda_q1[0], lambda_k1[0], lambda_q2[0], lambda_k2[0]]),
                   ((0, 4), (0, LANES - HEAD_DIM)))
    mod, lam = _mod_call(c8, w_ada[0], b_ada[0][None, :], lvec)
    mod3 = mod[:b].reshape(b, 3, d)
    lam = lam[0:1, 0:1]

    w = w_in[0]
    mq, mk, mv, dq, dk, dv = (w[:, i * kw:(i + 1) * kw] for i in range(6))
    wz = w[:, 6 * kw:]
    scale = HEAD_DIM ** -0.5 * float(np.log2(np.e))
    wa = jnp.concatenate([mk, dk, wz], axis=1).astype(BF16)
    wbt = jnp.concatenate([mq * scale, dq * scale, mv, dv], axis=1).T.astype(BF16)

    k_m, k_d, z, qt_m, qt_d, vt_m, vt_d, kmean = _proj_call(x, mod3, ln_g[0][None, :], wa, wbt)

    nb = s // BLK
    kmean_h = kmean.reshape(b, nb, MOBA_HEADS, HEAD_DIM).transpose(0, 2, 1, 3)
    selbias = _sel_call(kmean_h, qt_m)

    ms = _alibi_slopes(MOBA_HEADS)
    aug_m = _alibi_columns([ms[2 * p:2 * p + 2] for p in range(MOBA_HEADS // 2)], s)
    aug_d = _alibi_columns([[m] for m in _alibi_slopes(DIFF_HEADS)], s)

    a_out = _moba_call(qt_m, k_m, aug_m, vt_m, selbias)
    b_out = _diff_call(lam, qt_d, k_d, aug_d, vt_d, subln_g[0][:, None])
    out = _out_call(a_out, b_out, z, x, mod3, w_out[0].astype(BF16), final_g[None, :])
    return out, a_out, b_out


def kernel(x, c, ln_g, w_ada, b_ada, w_in, lambda_q1, lambda_k1, lambda_q2, lambda_k2, subln_g, w_out, final_g):
    return _forward(x, c, ln_g, w_ada, b_ada, w_in, lambda_q1, lambda_k1, lambda_q2, lambda_k2,
                    subln_g, w_out, final_g)[0]
```

```python
import functools

import numpy as np
import jax
import jax.numpy as jnp
from jax import lax
from jax.experimental import pallas as pl
from jax.experimental.pallas import tpu as pltpu

F32 = jnp.float32
BF16 = jnp.bfloat16

HEAD_DIM = 64
MOBA_HEADS = 8
DIFF_HEADS = 4
MOBA_BLOCK = 256
MOBA_TOPK = 3
EPS = 1e-6
LAM_INIT = 0.8 - 0.6 * float(np.exp(-0.3 * 0))

BLK = MOBA_BLOCK
TQ = 2 * BLK
LANES = 128
POS_SPLIT = 64
NEG = -1e30
VMEM_LIMIT = 48 * 1024 * 1024


def _alibi_slopes(n):
    return [2.0 ** (-8.0 * (i + 1) / n) for i in range(n)]


def _sigmoid(v):
    return 1.0 / (1.0 + jnp.exp(-v))


def _mod_kernel(c_ref, w_ref, b_ref, lv_ref, mod_ref, lam_ref):
    c = c_ref[...]
    ca = c * _sigmoid(c)
    ca_hi = ca.astype(BF16)
    ca_lo = (ca - ca_hi.astype(F32)).astype(BF16)
    w = w_ref[...]
    w_hi = w.astype(BF16)
    w_lo = (w - w_hi.astype(F32)).astype(BF16)
    acc = jnp.dot(ca_hi, w_hi, preferred_element_type=F32)
    acc += jnp.dot(ca_lo, w_hi, preferred_element_type=F32)
    acc += jnp.dot(ca_hi, w_lo, preferred_element_type=F32)
    mod_ref[...] = acc + b_ref[...]
    lv = lv_ref[...]
    s1 = jnp.sum(lv[0:1, :] * lv[1:2, :], axis=1, keepdims=True)
    s2 = jnp.sum(lv[2:3, :] * lv[3:4, :], axis=1, keepdims=True)
    lam = jnp.exp(s1) - jnp.exp(s2) + LAM_INIT
    lam_ref[...] = jnp.broadcast_to(lam, lam_ref.shape)


def _mod_call(c8, w_ada, b_ada, lvec):
    d = c8.shape[1]
    n = w_ada.shape[1]
    tn = 512
    return pl.pallas_call(
        _mod_kernel,
        grid=(n // tn,),
        in_specs=[
            pl.BlockSpec((8, d), lambda j: (0, 0)),
            pl.BlockSpec((d, tn), lambda j: (0, j)),
            pl.BlockSpec((1, tn), lambda j: (0, j)),
            pl.BlockSpec((8, LANES), lambda j: (0, 0)),
        ],
        out_specs=[
            pl.BlockSpec((8, tn), lambda j: (0, j)),
            pl.BlockSpec((8, LANES), lambda j: (0, 0)),
        ],
        out_shape=[
            jax.ShapeDtypeStruct((8, n), F32),
            jax.ShapeDtypeStruct((8, LANES), F32),
        ],
        compiler_params=pltpu.CompilerParams(dimension_semantics=("arbitrary",)),
        name="adaln_mod",
    )(c8, w_ada, b_ada, lvec)


def _proj_kernel(x_ref, mod_ref, g_ref, wa_ref, wbt_ref,
                 km_ref, kd_ref, z_ref, qm_ref, qd_ref, vm_ref, vd_ref, kmean_ref):
    x = x_ref[0]
    ms = jnp.mean(x * x, axis=-1, keepdims=True)
    y = x * lax.rsqrt(ms + EPS) * g_ref[...]
    shift = mod_ref[0, 0:1, :]
    scl = mod_ref[0, 1:2, :]
    h = (y * (1.0 + scl) + shift).astype(BF16)

    pa = jnp.dot(h, wa_ref[...], preferred_element_type=F32)
    kw = MOBA_HEADS * HEAD_DIM
    for p in range(kw // LANES):
        km_ref[0, p, 0] = pa[:, p * LANES:(p + 1) * LANES].astype(BF16)
        kd_ref[0, p, 0] = pa[:, kw + p * LANES: kw + (p + 1) * LANES].astype(BF16)
    z_ref[0] = pa[:, 2 * kw:].astype(BF16)
    kmean_ref[0, 0] = jnp.mean(pa[:, :kw], axis=0, keepdims=True)

    pb = lax.dot_general(wbt_ref[...], h, (((1,), (1,)), ((), ())), preferred_element_type=F32)
    qm_ref[0, 0] = pb[0 * kw:1 * kw].astype(BF16)
    qd_ref[0, 0] = pb[1 * kw:2 * kw].astype(BF16)
    vm_ref[0, 0] = pb[2 * kw:3 * kw].astype(BF16)
    vd_ref[0, 0] = pb[3 * kw:4 * kw].astype(BF16)


def _proj_call(x, mod3, ln_g, wa, wbt):
    b, s, d = x.shape
    nb = s // BLK
    kw = MOBA_HEADS * HEAD_DIM
    nsl = kw // LANES
    kslab = jax.ShapeDtypeStruct((b, nsl, nb, BLK, LANES), BF16)
    tslab = jax.ShapeDtypeStruct((b, nb, kw, BLK), BF16)
    kspec = pl.BlockSpec((1, nsl, 1, BLK, LANES), lambda i, t: (i, 0, t, 0, 0))
    tspec = pl.BlockSpec((1, 1, kw, BLK), lambda i, t: (i, t, 0, 0))
    return pl.pallas_call(
        _proj_kernel,
        grid=(b, nb),
        in_specs=[
            pl.BlockSpec((1, BLK, d), lambda i, t: (i, t, 0)),
            pl.BlockSpec((1, 3, d), lambda i, t: (i, 0, 0)),
            pl.BlockSpec((1, d), lambda i, t: (0, 0)),
            pl.BlockSpec(wa.shape, lambda i, t: (0, 0)),
            pl.BlockSpec(wbt.shape, lambda i, t: (0, 0)),
        ],
        out_specs=[
            kspec, kspec,
            pl.BlockSpec((1, BLK, d), lambda i, t: (i, t, 0)),
            tspec, tspec, tspec, tspec,
            pl.BlockSpec((1, 1, 1, kw), lambda i, t: (i, t, 0, 0)),
        ],
        out_shape=[
            kslab, kslab,
            jax.ShapeDtypeStruct((b, s, d), BF16),
            tslab, tslab, tslab, tslab,
            jax.ShapeDtypeStruct((b, nb, 1, kw), F32),
        ],
        compiler_params=pltpu.CompilerParams(
            dimension_semantics=("arbitrary", "arbitrary"), vmem_limit_bytes=VMEM_LIMIT),
        name="norm_in_proj",
    )(x, mod3, ln_g, wa, wbt)


SEL_BLOCKS = 8


def _sel_kernel(km_ref, q_ref, sb_ref, *, sel_blocks):
    t = pl.program_id(2)
    km = km_ref[0, 0]
    km_hi = km.astype(BF16)
    km_lo = (km - km_hi.astype(F32)).astype(BF16)
    nb = km.shape[0]
    gate = jnp.concatenate(
        [jnp.dot(km_hi, q_ref[0, u], preferred_element_type=F32)
         + jnp.dot(km_lo, q_ref[0, u], preferred_element_type=F32) for u in range(sel_blocks)],
        axis=1)
    blk = lax.broadcasted_iota(jnp.int32, gate.shape, 0)
    qblk = t * sel_blocks + lax.broadcasted_iota(jnp.int32, gate.shape, 1) // BLK
    valid = blk < qblk
    g = jnp.where(valid, gate, -jnp.inf)
    sel = jnp.zeros(gate.shape, jnp.bool_)
    for _ in range(MOBA_TOPK):
        mx = jnp.max(g, axis=0, keepdims=True)
        first = jnp.min(jnp.where(g == mx, blk, nb), axis=0, keepdims=True)
        hit = blk == first
        sel = jnp.logical_or(sel, hit)
        g = jnp.where(hit, -jnp.inf, g)
    sb_ref[0, 0] = jnp.where(jnp.logical_and(sel, valid), 0.0, NEG)


def _sel_call(kmean_h, qt_m):
    b, nh, nb, hd = kmean_h.shape
    sel_blocks = SEL_BLOCKS if nb % SEL_BLOCKS == 0 else 1
    return pl.pallas_call(
        functools.partial(_sel_kernel, sel_blocks=sel_blocks),
        grid=(b, nh, nb // sel_blocks),
        in_specs=[
            pl.BlockSpec((1, 1, nb, hd), lambda i, h, t: (i, h, 0, 0)),
            pl.BlockSpec((1, sel_blocks, hd, BLK), lambda i, h, t: (i, t, h, 0)),
        ],
        out_specs=pl.BlockSpec((1, 1, nb, sel_blocks * BLK), lambda i, h, t: (i, h, 0, t)),
        out_shape=jax.ShapeDtypeStruct((b, nh, nb, nb * BLK), F32),
        compiler_params=pltpu.CompilerParams(dimension_semantics=("arbitrary",) * 3),
        name="moba_select",
    )(kmean_h, qt_m)


def _bf16_split(value, parts):
    out, rest = [], np.float64(value)
    for _ in range(parts):
        piece = np.float32(rest).astype(BF16)
        out.append(float(piece))
        rest -= np.float64(piece)
    return out


LOG2E_PARTS = _bf16_split(np.log2(np.e), 3)
ALIBI_COLS = 2 * len(LOG2E_PARTS)
SUB = 16


def _alibi_rows(first_row):
    r = lax.broadcasted_iota(jnp.int32, (LANES, TQ), 0)
    out = jnp.zeros((LANES, TQ), F32)
    for i in range(ALIBI_COLS):
        out = jnp.where(r == first_row + i, LOG2E_PARTS[i % len(LOG2E_PARTS)], out)
    return out.astype(BF16)


def _ones_rows():
    r = lax.broadcasted_iota(jnp.int32, (SUB, BLK), 0)
    return jnp.where(r == 0, 1.0, 0.0).astype(BF16)


M_INIT = -1e20
NCOL = 2 * TQ


class _Stream:
    def __init__(self, scores, pv, bias, acc_refs, acc_cols, s_buf, p_buf):
        self.scores, self.pv, self.bias = scores, pv, bias
        self.acc_refs, self.acc_cols, self.s_buf, self.p_buf = acc_refs, acc_cols, s_buf, p_buf


def _flash(qi, streams):
    def colmax(s):
        return jnp.max(s, axis=0, keepdims=True)

    def soft(st, slot, cm, b, m):
        if b is not None:
            cm = cm + b
        m_new = jnp.maximum(m, cm)
        alpha = jnp.exp2(m - m_new)
        shift = (-m_new) if b is None else (b - m_new)
        for r in range(0, BLK, SUB):
            st.p_buf[slot, r:r + SUB, :] = jnp.exp2(st.s_buf[slot, r:r + SUB, :] + shift).astype(BF16)
        return m_new, alpha

    def accumulate(st, alpha1, parts1, alpha2, parts2):
        for ref, d1, d2, c in zip(st.acc_refs, parts1, parts2, st.acc_cols):
            ref[...] = alpha1[:, c] * ref[...] + d1
            ref[...] = alpha2[:, c] * ref[...] + d2

    def visible(first_row):
        r = lax.broadcasted_iota(jnp.int32, (BLK, NCOL), 0)
        c = lax.broadcasted_iota(jnp.int32, (BLK, NCOL), 1)
        return r + first_row <= jnp.bitwise_and(c, TQ - 1)

    def pair_one(st, i, carry, last):
        m, alpha_p, alpha_q, cm_a = carry
        a = 2 * i
        b = a + 1
        accumulate(st, alpha_p, st.pv(jnp.maximum(a - 2, 0), st.p_buf[0]),
                   alpha_q, st.pv(jnp.maximum(a - 1, 0), st.p_buf[1]))
        sb = st.scores(b)
        if last:
            sb = jnp.where(visible(BLK), sb, NEG)
        st.s_buf[1] = sb
        cm_b = colmax(sb)
        ba = None if st.bias is None else st.bias(a)
        if last:
            sa = jnp.where(visible(0), st.s_buf[0], NEG)
            st.s_buf[0] = sa
            cm_a = colmax(sa)
            if ba is not None:
                cq = jnp.bitwise_and(lax.broadcasted_iota(jnp.int32, (1, NCOL), 1), TQ - 1)
                ba = jnp.where(cq < BLK, 0.0, ba)
        m, alpha_a = soft(st, 0, cm_a, ba, m)
        cm_n = cm_b
        if not last:
            sn = st.scores(b + 1)
            st.s_buf[0] = sn
            cm_n = colmax(sn)
        bb = None if (st.bias is None or last) else st.bias(b)
        m, alpha_b = soft(st, 1, cm_b, bb, m)
        return m, alpha_a, alpha_b, cm_n

    def pair(i, carries, last):
        return tuple(pair_one(st, i, c, last) for st, c in zip(streams, carries))

    one = jnp.ones((1, NCOL), F32)
    carries = []
    for st in streams:
        for ref in st.acc_refs:
            ref[...] = jnp.zeros(ref.shape, F32)
        s0 = st.scores(0)
        st.s_buf[0] = s0
        st.p_buf[...] = jnp.zeros(st.p_buf.shape, BF16)
        carries.append((jnp.full((1, NCOL), M_INIT, F32), one, one, colmax(s0)))
    carries = tuple(carries)
    carries = lax.fori_loop(0, qi // 2, lambda i, c: pair(2 * i + 1, pair(2 * i, c, False), False), carries)
    carries = lax.fori_loop(0, qi % 2, lambda _, c: pair(qi - 1, c, False), carries)
    carries = pair(qi, carries, True)
    for st, (_, alpha_a, alpha_b, _) in zip(streams, carries):
        accumulate(st, alpha_a, st.pv(2 * qi, st.p_buf[0]), alpha_b, st.pv(2 * qi + 1, st.p_buf[1]))


NSTREAM = 2


def _moba_kernel(q_ref, k_ref, aug_ref, v_ref, sb_ref, o_ref, s_buf, p_buf, acc_ref):
    qi = pl.program_id(2)
    ones = _ones_rows()
    zero = jnp.zeros((HEAD_DIM, TQ), BF16)
    halves = (slice(0, TQ), slice(TQ, NCOL))

    def stream(g):
        rows = slice(g * LANES, (g + 1) * LANES)
        q = jnp.concatenate([q_ref[0, 0, rows], q_ref[0, 1, rows]], axis=1)
        qp = jnp.concatenate([
            jnp.concatenate([q[0:HEAD_DIM], zero, _alibi_rows(0)], axis=0),
            jnp.concatenate([zero, q[HEAD_DIM:], _alibi_rows(ALIBI_COLS)], axis=0)], axis=1)

        def scores(j):
            kd = jnp.concatenate([k_ref[0, g, j], aug_ref[g, j]], axis=1)
            return jnp.dot(kd, qp, preferred_element_type=F32)

        def pv(j, pb):
            v = v_ref[0, j, rows]
            va = jnp.concatenate([v[0:HEAD_DIM], ones, v[HEAD_DIM:]], axis=0)
            return (jnp.dot(va[0:HEAD_DIM + SUB], pb[:, :TQ], preferred_element_type=F32),
                    jnp.dot(va[HEAD_DIM:], pb[:, TQ:], preferred_element_type=F32))

        def bias(j):
            return jnp.concatenate([sb_ref[0, 2 * g, pl.ds(j, 1), :], sb_ref[0, 2 * g + 1, pl.ds(j, 1), :]], axis=1)

        return _Stream(scores, pv, bias, (acc_ref.at[2 * g], acc_ref.at[2 * g + 1]), halves,
                       s_buf.at[g], p_buf.at[g])

    _flash(qi, [stream(g) for g in range(NSTREAM)])
    outs = []
    for g in range(NSTREAM):
        acc0 = acc_ref[2 * g]
        acc1 = acc_ref[2 * g + 1]
        outs += [acc0[0:HEAD_DIM] / acc0[HEAD_DIM:HEAD_DIM + 1], acc1[SUB:] / acc1[0:1]]
    o_ref[0] = jnp.concatenate(outs, axis=0).T.astype(BF16)


def _moba_call(qt_m, k_m, aug_m, vt_m, selbias):
    b, nb, kw, _ = qt_m.shape
    npair = kw // LANES
    s = nb * BLK
    ns = NSTREAM
    return pl.pallas_call(
        _moba_kernel,
        grid=(b, npair // ns, s // TQ),
        in_specs=[
            pl.BlockSpec((1, TQ // BLK, ns * LANES, BLK), lambda i, p, t: (i, t, p, 0)),
            pl.BlockSpec((1, ns, nb, BLK, LANES), lambda i, p, t: (i, p, 0, 0, 0)),
            pl.BlockSpec((ns, nb, BLK, LANES), lambda i, p, t: (p, 0, 0, 0)),
            pl.BlockSpec((1, nb, ns * LANES, BLK), lambda i, p, t: (i, 0, p, 0)),
            pl.BlockSpec((1, 2 * ns, nb, TQ), lambda i, p, t: (i, p, 0, t)),
        ],
        out_specs=pl.BlockSpec((1, TQ, ns * LANES), lambda i, p, t: (i, t, p)),
        out_shape=jax.ShapeDtypeStruct((b, s, kw), BF16),
        scratch_shapes=[
            pltpu.VMEM((ns, 2, BLK, NCOL), F32),
            pltpu.VMEM((ns, 2, BLK, NCOL), BF16),
            pltpu.VMEM((2 * ns, HEAD_DIM + SUB, TQ), F32),
        ],
        compiler_params=pltpu.CompilerParams(
            dimension_semantics=("arbitrary",) * 3, vmem_limit_bytes=VMEM_LIMIT),
        name="moba_attn",
    )(qt_m, k_m, aug_m, vt_m, selbias)


def _diff_kernel(lam_ref, q_ref, k_ref, aug_ref, v_ref, g_ref, o_ref, s_buf, p_buf, acc_ref):
    qi = pl.program_id(2)
    ones = _ones_rows()
    zero = jnp.zeros((HEAD_DIM, TQ), BF16)
    arows = _alibi_rows(0)

    def stream(g):
        rows = slice(g * LANES, (g + 1) * LANES)
        q = jnp.concatenate([q_ref[0, 0, rows], q_ref[0, 1, rows]], axis=1)
        qp = jnp.concatenate([
            jnp.concatenate([q[0:HEAD_DIM], zero, arows], axis=0),
            jnp.concatenate([zero, q[HEAD_DIM:], arows], axis=0)], axis=1)

        def scores(j):
            kd = jnp.concatenate([k_ref[0, g, j], aug_ref[g, j]], axis=1)
            return jnp.dot(kd, qp, preferred_element_type=F32)

        def pv(j, pb):
            va = jnp.concatenate([v_ref[0, j, rows], ones], axis=0)
            return (jnp.dot(va, pb, preferred_element_type=F32),)

        return _Stream(scores, pv, None, (acc_ref.at[g],), (slice(0, NCOL),), s_buf.at[g], p_buf.at[g])

    _flash(qi, [stream(g) for g in range(NSTREAM)])
    outs = []
    for g in range(NSTREAM):
        acc = acc_ref[g]
        o = acc[0:2 * HEAD_DIM] / acc[2 * HEAD_DIM:2 * HEAD_DIM + 1]
        d = o[:, :TQ] - lam_ref[0, 0] * o[:, TQ:]
        ms = jnp.mean(d * d, axis=0, keepdims=True)
        outs.append(d * lax.rsqrt(ms + EPS) * g_ref[...] * (1.0 - LAM_INIT))
    o_ref[0] = jnp.concatenate(outs, axis=0).T.astype(BF16)


def _diff_call(lam, qt_d, k_d, aug_d, vt_d, subln_col):
    b, nb, kw, _ = qt_d.shape
    nh = kw // LANES
    s = nb * BLK
    ns = NSTREAM
    return pl.pallas_call(
        _diff_kernel,
        grid=(b, nh // ns, s // TQ),
        in_specs=[
            pl.BlockSpec(memory_space=pltpu.SMEM),
            pl.BlockSpec((1, TQ // BLK, ns * LANES, BLK), lambda i, h, t: (i, t, h, 0)),
            pl.BlockSpec((1, ns, nb, BLK, LANES), lambda i, h, t: (i, h, 0, 0, 0)),
            pl.BlockSpec((ns, nb, BLK, LANES), lambda i, h, t: (h, 0, 0, 0)),
            pl.BlockSpec((1, nb, ns * LANES, BLK), lambda i, h, t: (i, 0, h, 0)),
            pl.BlockSpec((LANES, 1), lambda i, h, t: (0, 0)),
        ],
        out_specs=pl.BlockSpec((1, TQ, ns * LANES), lambda i, h, t: (i, t, h)),
        out_shape=jax.ShapeDtypeStruct((b, s, kw), BF16),
        scratch_shapes=[
            pltpu.VMEM((ns, 2, BLK, NCOL), F32),
            pltpu.VMEM((ns, 2, BLK, NCOL), BF16),
            pltpu.VMEM((ns, 2 * HEAD_DIM + SUB, NCOL), F32),
        ],
        compiler_params=pltpu.CompilerParams(
            dimension_semantics=("arbitrary",) * 3, vmem_limit_bytes=VMEM_LIMIT),
        name="diff_attn",
    )(lam, qt_d, k_d, aug_d, vt_d, subln_col)


def _out_kernel(a_ref, b_ref, z_ref, x_ref, mod_ref, w_ref, g_ref, o_ref):
    z = z_ref[0].astype(F32)
    heads = jnp.concatenate([a_ref[0], b_ref[0]], axis=-1).astype(F32)
    y = (heads * (z * _sigmoid(z))).astype(BF16)
    r = jnp.dot(y, w_ref[...], preferred_element_type=F32)
    xo = x_ref[0] + mod_ref[0, 2:3, :] * r
    ms = jnp.mean(xo * xo, axis=-1, keepdims=True)
    o_ref[0] = xo * lax.rsqrt(ms + EPS) * g_ref[...]


def _out_call(a_out, b_out, z, x, mod3, w_out, final_g):
    b, s, d = x.shape
    tm = 512
    kw = a_out.shape[-1]
    return pl.pallas_call(
        _out_kernel,
        grid=(b, s // tm),
        in_specs=[
            pl.BlockSpec((1, tm, kw), lambda i, t: (i, t, 0)),
            pl.BlockSpec((1, tm, kw), lambda i, t: (i, t, 0)),
            pl.BlockSpec((1, tm, d), lambda i, t: (i, t, 0)),
            pl.BlockSpec((1, tm, d), lambda i, t: (i, t, 0)),
            pl.BlockSpec((1, 3, d), lambda i, t: (i, 0, 0)),
            pl.BlockSpec(w_out.shape, lambda i, t: (0, 0)),
            pl.BlockSpec((1, d), lambda i, t: (0, 0)),
        ],
        out_specs=pl.BlockSpec((1, tm, d), lambda i, t: (i, t, 0)),
        out_shape=jax.ShapeDtypeStruct((b, s, d), F32),
        compiler_params=pltpu.CompilerParams(
            dimension_semantics=("arbitrary", "arbitrary"), vmem_limit_bytes=VMEM_LIMIT),
        name="gate_out_proj_norm",
    )(a_out, b_out, z, x, mod3, w_out, final_g)


def _alibi_columns(slopes_per_slab, s):
    pos = np.arange(s)
    hi = (pos // POS_SPLIT) * POS_SPLIT
    lo = pos % POS_SPLIT
    half = ALIBI_COLS // 2
    out = np.zeros((len(slopes_per_slab), s, LANES), np.float32)
    for n, slopes in enumerate(slopes_per_slab):
        for i, m in enumerate(slopes):
            out[n, :, ALIBI_COLS * i:ALIBI_COLS * i + half] = (m * hi)[:, None]
            out[n, :, ALIBI_COLS * i + half:ALIBI_COLS * (i + 1)] = (m * lo)[:, None]
    return jnp.asarray(out.reshape(len(slopes_per_slab), s // BLK, BLK, LANES), dtype=BF16)


def _forward(x, c, ln_g, w_ada, b_ada, w_in, lambda_q1, lambda_k1, lambda_q2, lambda_k2, subln_g, w_out, final_g):
    b, s, d = x.shape
    kw = MOBA_HEADS * HEAD_DIM
    assert s % TQ == 0 and d == 2 * kw and w_in.shape[-1] == 6 * kw + d

    c8 = jnp.pad(c, ((0, 8 - b), (0, 0)))
    lvec = jnp.pad(jnp.stack([lambda_q1[0], lambda_k1[0], lambda_q2[0], lambda_k2[0]]),
                   ((0, 4), (0, LANES - HEAD_DIM)))
    mod, lam = _mod_call(c8, w_ada[0], b_ada[0][None, :], lvec)
    mod3 = mod[:b].reshape(b, 3, d)
    lam = lam[0:1, 0:1]

    w = w_in[0]
    mq, mk, mv, dq, dk, dv = (w[:, i * kw:(i + 1) * kw] for i in range(6))
    wz = w[:, 6 * kw:]
    scale = HEAD_DIM ** -0.5 * float(np.log2(np.e))
    wa = jnp.concatenate([mk, dk, wz], axis=1).astype(BF16)
    wbt = jnp.concatenate([mq * scale, dq * scale, mv, dv], axis=1).T.astype(BF16)

    k_m, k_d, z, qt_m, qt_d, vt_m, vt_d, kmean = _proj_call(x, mod3, ln_g[0][None, :], wa, wbt)

    nb = s // BLK
    kmean_h = kmean.reshape(b, nb, MOBA_HEADS, HEAD_DIM).transpose(0, 2, 1, 3)
    selbias = _sel_call(kmean_h, qt_m)

    ms = _alibi_slopes(MOBA_HEADS)
    aug_m = _alibi_columns([ms[2 * p:2 * p + 2] for p in range(MOBA_HEADS // 2)], s)
    aug_d = _alibi_columns([[m] for m in _alibi_slopes(DIFF_HEADS)], s)

    a_out = _moba_call(qt_m, k_m, aug_m, vt_m, selbias)
    b_out = _diff_call(lam, qt_d, k_d, aug_d, vt_d, subln_g[0][:, None])
    out = _out_call(a_out, b_out, z, x, mod3, w_out[0].astype(BF16), final_g[None, :])
    return out, a_out, b_out


def kernel(x, c, ln_g, w_ada, b_ada, w_in, lambda_q1, lambda_k1, lambda_q2, lambda_k2, subln_g, w_out, final_g):
    return _forward(x, c, ln_g, w_ada, b_ada, w_in, lambda_q1, lambda_k1, lambda_q2, lambda_k2,
                    subln_g, w_out, final_g)[0]
```
